```python
import math
import jax, jax.numpy as jnp
from jax import lax
import numpy as np

D_MODEL = 1024
BATCH = 8
SEQ = 4096
DEPTH = 2

CTX_LEN = 256
GRID_W = 64
N_HEADS = 8
QK_NOPE = 64
QK_ROPE = 32
V_DIM = 64
Q_LORA = 256
KV_LORA = 256
ROPE_THETA = 10000.0
Q_BLOCK = 128
ATTN_SCALE = 1.0 / math.sqrt(QK_NOPE + QK_ROPE)
CONV_WIDTH = 512
CONV_K = 3
S5_WIDTH = 512
S5_GROUP = 16
S5_GROUPS = S5_WIDTH // S5_GROUP
S5_STATE = 64
DT_MIN = 1e-3
DT_MAX = 1e-1
N_BRANCH = 3
D_FF = 4 * D_MODEL
EPS = 1e-6
OFF_Q = 0
OFF_KV = OFF_Q + Q_LORA
OFF_PE = OFF_KV + KV_LORA
OFF_CB = OFF_PE + QK_ROPE
OFF_CC = OFF_CB + CONV_WIDTH
OFF_CX = OFF_CC + CONV_WIDTH
OFF_S5 = OFF_CX + CONV_WIDTH
OFF_G = OFF_S5 + S5_WIDTH
IN_COLS = OFF_G + N_BRANCH * D_MODEL

kernel_name = 'hybrid_mla_shortconv_s5_dit_block'


def rms_norm(x, g):
    xf = x.astype(jnp.float32)
    y = xf * lax.rsqrt(jnp.mean(xf * xf, axis=-1, keepdims=True) + EPS)
    return (y * g.astype(jnp.float32)).astype(x.dtype)


def modulate(x, shift, scale):
    return x * (1 + scale) + shift


def squared_relu_mlp(x, w1, w2):
    return jnp.square(jax.nn.relu(x @ w1)) @ w2


def axial_rope_tables(n_tokens, dtype):
    rows = n_tokens // GRID_W
    row = jnp.repeat(jnp.arange(rows), GRID_W)
    col = jnp.tile(jnp.arange(GRID_W), rows)
    pos = jnp.stack([row, col], axis=-1).astype(jnp.float32)
    n_freq = QK_ROPE // 4
    inv = ROPE_THETA ** (-jnp.arange(n_freq, dtype=jnp.float32) / n_freq)
    ang = pos[:, :, None, None] * inv[None, None, None, :]
    ang = jnp.broadcast_to(ang, (n_tokens, 2, 2, n_freq)).reshape(n_tokens, QK_ROPE)
    return jnp.cos(ang).astype(dtype), jnp.sin(ang).astype(dtype)


def apply_rope(x, cos, sin):
    xr = x.reshape(x.shape[:-1] + (2, 2, QK_ROPE // 4))
    rot = jnp.concatenate([-xr[..., 1:, :], xr[..., :1, :]], axis=-2).reshape(x.shape)
    return x * cos + rot * sin


def mla_queries(z_q, q_norm, w_uq, rope):
    b, n, _ = z_q.shape
    q = (rms_norm(z_q, q_norm) @ w_uq).reshape(b, n, N_HEADS, QK_NOPE + QK_ROPE).transpose(0, 2, 1, 3)
    q_nope, q_pe = q[..., :QK_NOPE], q[..., QK_NOPE:]
    if rope is not None:
        q_pe = apply_rope(q_pe, rope[0], rope[1])
    return jnp.concatenate([q_nope, q_pe], axis=-1)


def mla_keys_values(z_kv, z_pe, kv_norm, w_ukv, rope):
    b, n, _ = z_kv.shape
    kv = (rms_norm(z_kv, kv_norm) @ w_ukv).reshape(b, n, N_HEADS, QK_NOPE + V_DIM).transpose(0, 2, 1, 3)
    k_nope, v = kv[..., :QK_NOPE], kv[..., QK_NOPE:]
    k_pe = z_pe[:, None]
    if rope is not None:
        k_pe = apply_rope(k_pe, rope[0], rope[1])
    k = jnp.concatenate([k_nope, jnp.broadcast_to(k_pe, (b, N_HEADS, n, QK_ROPE))], axis=-1)
    return k, v


def softmax_attend(q, k, v):
    s = jnp.einsum('bhqd,bhkd->bhqk', q, k).astype(jnp.float32) * ATTN_SCALE
    p = jax.nn.softmax(s, axis=-1).astype(v.dtype)
    return jnp.einsum('bhqk,bhkd->bhqd', p, v)


def blocked_attend(q, k, v):
    b, h, n, dk = q.shape
    nb = n // Q_BLOCK
    qb = q.reshape(b, h, nb, Q_BLOCK, dk).transpose(2, 0, 1, 3, 4)
    ob = lax.map(lambda qi: softmax_attend(qi, k, v), qb)
    return ob.transpose(1, 2, 0, 3, 4).reshape(b, h, n, V_DIM)


def merge_heads(o):
    b, h, n, d = o.shape
    return o.transpose(0, 2, 1, 3).reshape(b, n, h * d)


def short_conv(z_b, z_c, z_x, conv_w, conv_w_out):
    u = z_c * z_x
    n = u.shape[1]
    pad = CONV_K // 2
    up = jnp.pad(u, ((0, 0), (pad, pad), (0, 0)))
    y = up[:, 0:n] * conv_w[0]
    for j in range(1, CONV_K):
        y = y + up[:, j:j + n] * conv_w[j]
    return (z_b * y) @ conv_w_out


def cmul(ar, ai, br, bi):
    return ar * br - ai * bi, ar * bi + ai * br


def s5_discretise(a_re, a_im, log_dt, b_re, b_im):
    dt = jnp.exp(log_dt.astype(jnp.float32))[:, None]
    a_re = a_re.astype(jnp.float32)
    a_im = a_im.astype(jnp.float32)
    mag = jnp.exp(dt * a_re)
    ab_re, ab_im = mag * jnp.cos(dt * a_im), mag * jnp.sin(dt * a_im)
    den = a_re * a_re + a_im * a_im
    nr, ni = ab_re - 1.0, ab_im
    f_re = (nr * a_re + ni * a_im) / den
    f_im = (ni * a_re - nr * a_im) / den
    bb_re, bb_im = cmul(f_re[..., None], f_im[..., None], b_re.astype(jnp.float32), b_im.astype(jnp.float32))
    return ab_re, ab_im, bb_re, bb_im


def diag_scan(ab_re, ab_im, b_re, b_im):
    n = b_re.shape[1]
    a_re = jnp.broadcast_to(ab_re, (1, n) + ab_re.shape)
    a_im = jnp.broadcast_to(ab_im, (1, n) + ab_im.shape)

    def combine(e1, e2):
        a1r, a1i, b1r, b1i = e1
        a2r, a2i, b2r, b2i = e2
        ar, ai = cmul(a2r, a2i, a1r, a1i)
        br, bi = cmul(a2r, a2i, b1r, b1i)
        return ar, ai, br + b2r, bi + b2i

    _, _, s_re, s_im = lax.associative_scan(combine, (a_re, a_im, b_re, b_im), axis=1)
    return s_re, s_im


def s5_direction(u_c, u_l, a_re, a_im, log_dt, b_re, b_im, c_re, c_im, reverse, need_ctx_out):
    ab_re, ab_im, bb_re, bb_im = s5_discretise(a_re, a_im, log_dt, b_re, b_im)
    c_re = c_re.astype(jnp.float32)
    c_im = c_im.astype(jnp.float32)

    def flip(t):
        return t[:, ::-1] if reverse else t

    def drive(u):
        return jnp.einsum('blgc,gpc->blgp', u, bb_re), jnp.einsum('blgc,gpc->blgp', u, bb_im)

    def readout(s_re, s_im):
        return jnp.einsum('gcp,blgp->blgc', c_re, s_re) - jnp.einsum('gcp,blgp->blgc', c_im, s_im)

    bc_re, bc_im = drive(flip(u_c))
    sc_re, sc_im = diag_scan(ab_re, ab_im, bc_re, bc_im)
    i_re, i_im = cmul(ab_re, ab_im, sc_re[:, -1], sc_im[:, -1])
    bl_re, bl_im = drive(flip(u_l))
    bl_re = bl_re.at[:, 0].add(i_re)
    bl_im = bl_im.at[:, 0].add(i_im)
    sl_re, sl_im = diag_scan(ab_re, ab_im, bl_re, bl_im)
    y_l = flip(readout(sl_re, sl_im))
    y_c = flip(readout(sc_re, sc_im)) if need_ctx_out else None
    return y_l, y_c


def s5_branch(u_c, u_l, a_re, a_im, log_dt, b_re, b_im, c_re, c_im, s5_d, w_glu, need_ctx_out):
    dtype = u_l.dtype

    def groups(u):
        return u.astype(jnp.float32).reshape(u.shape[0], u.shape[1], S5_GROUPS, S5_GROUP)

    gc, gl = groups(u_c), groups(u_l)
    d = s5_d.astype(jnp.float32).reshape(S5_GROUPS, S5_GROUP)
    y_l = d * gl
    y_c = d * gc if need_ctx_out else None
    for direction in range(2):
        yl_d, yc_d = s5_direction(gc, gl, a_re[direction], a_im[direction], log_dt[direction],
                                  b_re[direction], b_im[direction], c_re[direction], c_im[direction],
                                  direction == 1, need_ctx_out)
        y_l = y_l + yl_d
        if need_ctx_out:
            y_c = y_c + yc_d

    def glu(y):
        z = jax.nn.gelu(y.reshape(y.shape[0], y.shape[1], S5_WIDTH)).astype(dtype) @ w_glu
        return z[..., :D_MODEL] * jax.nn.sigmoid(z[..., D_MODEL:])

    return glu(y_l), (glu(y_c) if need_ctx_out else None)


def gated_merge(z_gate, y_att, y_conv, y_s5, w_out):
    b, n, _ = z_gate.shape
    g = jax.nn.sigmoid(z_gate).reshape(b, n, N_BRANCH, D_MODEL)
    return (g[:, :, 0] * y_att + g[:, :, 1] * y_conv + g[:, :, 2] * y_s5) @ w_out


def token_mixers(xn, hn, w_in, q_norm, w_uq, kv_norm, w_ukv, w_o, conv_w, conv_w_out,
                 a_re, a_im, log_dt, b_re, b_im, c_re, c_im, s5_d, w_glu, w_out, rope, need_ctx_out):
    zx = xn @ w_in
    if need_ctx_out:
        zh = hn @ w_in
        zh_kvpe, zh_s5 = zh[..., OFF_KV:OFF_CB], zh[..., OFF_S5:OFF_G]
    else:
        zh_kvpe = hn @ w_in[:, OFF_KV:OFF_CB]
        zh_s5 = hn @ w_in[:, OFF_S5:OFF_G]
    k_c, v_c = mla_keys_values(zh_kvpe[..., :KV_LORA], zh_kvpe[..., KV_LORA:], kv_norm, w_ukv, None)
    k_x, v_x = mla_keys_values(zx[..., OFF_KV:OFF_PE], zx[..., OFF_PE:OFF_CB], kv_norm, w_ukv, rope)
    q_x = mla_queries(zx[..., OFF_Q:OFF_KV], q_norm, w_uq, rope)
    k_all = jnp.concatenate([k_c, k_x], axis=2)
    v_all = jnp.concatenate([v_c, v_x], axis=2)
    att_x = merge_heads(blocked_attend(q_x, k_all, v_all)) @ w_o
    conv_x = short_conv(zx[..., OFF_CB:OFF_CC], zx[..., OFF_CC:OFF_CX], zx[..., OFF_CX:OFF_S5], conv_w, conv_w_out)
    s5_x, s5_h = s5_branch(zh_s5, zx[..., OFF_S5:OFF_G], a_re, a_im, log_dt, b_re, b_im, c_re, c_im,
                           s5_d, w_glu, need_ctx_out)
    y_x = gated_merge(zx[..., OFF_G:], att_x, conv_x, s5_x, w_out)
    if not need_ctx_out:
        return y_x, None
    q_h = mla_queries(zh[..., OFF_Q:OFF_KV], q_norm, w_uq, None)
    att_h = merge_heads(softmax_attend(q_h, k_c, v_c)) @ w_o
    conv_h = short_conv(zh[..., OFF_CB:OFF_CC], zh[..., OFF_CC:OFF_CX], zh[..., OFF_CX:OFF_S5], conv_w, conv_w_out)
    y_h = gated_merge(zh[..., OFF_G:], att_h, conv_h, s5_h, w_out)
    return y_x, y_h


def setup_inputs(seed: int = 0) -> dict:
    key = jax.random.key(seed)
    ks = jax.random.split(key, 32)
    f32 = jnp.float32

    def nrm(k, shape, scale):
        return jax.random.normal(k, shape, f32) * scale

    def gain(k, shape):
        return 1.0 + 0.02 * jax.random.normal(k, shape, f32)

    n_idx = jnp.arange(S5_STATE, dtype=f32)
    s5_a_re = -0.5 + 0.01 * jax.random.normal(ks[14], (DEPTH, 2, S5_GROUPS, S5_STATE), f32)
    s5_a_im = math.pi * n_idx + 0.01 * jax.random.normal(ks[15], (DEPTH, 2, S5_GROUPS, S5_STATE), f32)
    s5_log_dt = jax.random.uniform(ks[16], (DEPTH, 2, S5_GROUPS), f32, math.log(DT_MIN), math.log(DT_MAX))
    return {
        'x': nrm(ks[0], (BATCH, SEQ, D_MODEL), 1.0),
        'c': nrm(ks[1], (BATCH, D_MODEL), 1.0),
        'ctx': nrm(ks[2], (BATCH, CTX_LEN, D_MODEL), 1.0),
        'c_ctx': nrm(ks[3], (D_MODEL,), 1.0),
        'ada_w': nrm(ks[4], (DEPTH, D_MODEL, 6 * D_MODEL), 0.02),
        'ada_b': nrm(ks[5], (DEPTH, 6 * D_MODEL), 0.01),
        'norm_mix': gain(ks[6], (DEPTH, D_MODEL)),
        'w_in': nrm(ks[7], (DEPTH, D_MODEL, IN_COLS), D_MODEL ** -0.5),
        'mla_q_norm': gain(ks[8], (DEPTH, Q_LORA)),
        'mla_w_uq': nrm(ks[9], (DEPTH, Q_LORA, N_HEADS * (QK_NOPE + QK_ROPE)), Q_LORA ** -0.5),
        'mla_kv_norm': gain(ks[10], (DEPTH, KV_LORA)),
        'mla_w_ukv': nrm(ks[11], (DEPTH, KV_LORA, N_HEADS * (QK_NOPE + V_DIM)), KV_LORA ** -0.5),
        'mla_w_o': nrm(ks[12], (DEPTH, N_HEADS * V_DIM, D_MODEL), (N_HEADS * V_DIM) ** -0.5),
        'conv_w': nrm(ks[13], (DEPTH, CONV_K, CONV_WIDTH), CONV_K ** -0.5),
        'conv_w_out': nrm(ks[17], (DEPTH, CONV_WIDTH, D_MODEL), CONV_WIDTH ** -0.5),
        's5_a_re': s5_a_re,
        's5_a_im': s5_a_im,
        's5_log_dt': s5_log_dt,
        's5_b_re': nrm(ks[18], (DEPTH, 2, S5_GROUPS, S5_STATE, S5_GROUP), (2 * S5_GROUP) ** -0.5),
        's5_b_im': nrm(ks[19], (DEPTH, 2, S5_GROUPS, S5_STATE, S5_GROUP), (2 * S5_GROUP) ** -0.5),
        's5_c_re': nrm(ks[20], (DEPTH, 2, S5_GROUPS, S5_GROUP, S5_STATE), (2 * S5_STATE) ** -0.5),
        's5_c_im': nrm(ks[21], (DEPTH, 2, S5_GROUPS, S5_GROUP, S5_STATE), (2 * S5_STATE) ** -0.5),
        's5_d': nrm(ks[22], (DEPTH, S5_WIDTH), 1.0),
        's5_w_glu': nrm(ks[23], (DEPTH, S5_WIDTH, 2 * D_MODEL), S5_WIDTH ** -0.5),
        'w_out': nrm(ks[24], (DEPTH, D_MODEL, D_MODEL), D_MODEL ** -0.5),
        'norm_mlp': gain(ks[25], (DEPTH, D_MODEL)),
        'mlp_w1': nrm(ks[26], (DEPTH, D_MODEL, D_FF), D_MODEL ** -0.5),
        'mlp_w2': nrm(ks[27], (DEPTH, D_FF, D_MODEL), D_FF ** -0.5),
        'norm_final': gain(ks[28], (D_MODEL,)),
    }


def reference(x, c, ctx, c_ctx, ada_w, ada_b, norm_mix, w_in, mla_q_norm, mla_w_uq, mla_kv_norm, mla_w_ukv,
              mla_w_o, conv_w, conv_w_out, s5_a_re, s5_a_im, s5_log_dt, s5_b_re, s5_b_im, s5_c_re, s5_c_im,
              s5_d, s5_w_glu, w_out, norm_mlp, mlp_w1, mlp_w2, norm_final):
    n_tokens = x.shape[1]
    rope = axial_rope_tables(n_tokens, x.dtype)
    silu_c = jax.nn.silu(c)
    silu_cc = jax.nn.silu(c_ctx)
    h = ctx
    for i in range(DEPTH):
        need_ctx_out = i < DEPTH - 1
        mod_x = jnp.split((silu_c @ ada_w[i] + ada_b[i])[:, None, :], 6, axis=-1)
        mod_h = jnp.split((silu_cc @ ada_w[i] + ada_b[i])[None, None, :], 6, axis=-1)
        xn = modulate(rms_norm(x, norm_mix[i]), mod_x[0], mod_x[1])
        hn = modulate(rms_norm(h, norm_mix[i]), mod_h[0], mod_h[1])
        y_x, y_h = token_mixers(xn, hn, w_in[i], mla_q_norm[i], mla_w_uq[i], mla_kv_norm[i], mla_w_ukv[i],
                                mla_w_o[i], conv_w[i], conv_w_out[i], s5_a_re[i], s5_a_im[i], s5_log_dt[i],
                                s5_b_re[i], s5_b_im[i], s5_c_re[i], s5_c_im[i], s5_d[i], s5_w_glu[i], w_out[i],
                                rope, need_ctx_out)
        x = x + mod_x[2] * y_x
        x = x + mod_x[5] * squared_relu_mlp(modulate(rms_norm(x, norm_mlp[i]), mod_x[3], mod_x[4]),
                                            mlp_w1[i], mlp_w2[i])
        if need_ctx_out:
            h = h + mod_h[2] * y_h
            h = h + mod_h[5] * squared_relu_mlp(modulate(rms_norm(h, norm_mlp[i]), mod_h[3], mod_h[4]),
                                                mlp_w1[i], mlp_w2[i])
    return rms_norm(x, norm_final)
```

```python
import functools
import math

import jax
import jax.numpy as jnp
from jax import lax
from jax.experimental import pallas as pl
from jax.experimental.pallas import tpu as pltpu

F32 = jnp.float32
BF16 = jnp.bfloat16

D_MODEL = 1024
GRID_W = 64
N_HEADS = 8
QK_NOPE = 64
QK_ROPE = 32
V_DIM = 64
Q_LORA = 256
KV_LORA = 256
ROPE_THETA = 10000.0
ATTN_SCALE = 1.0 / math.sqrt(QK_NOPE + QK_ROPE)
CONV_WIDTH = 512
CONV_K = 3
S5_WIDTH = 512
S5_GROUP = 16
S5_GROUPS = S5_WIDTH // S5_GROUP
S5_STATE = 64
N_BRANCH = 3
D_FF = 4 * D_MODEL
EPS = 1e-6
N_MOD = 6

OFF_Q = 0
OFF_KV = OFF_Q + Q_LORA
OFF_PE = OFF_KV + KV_LORA
OFF_CB = OFF_PE + QK_ROPE
OFF_CC = OFF_CB + CONV_WIDTH
OFF_CX = OFF_CC + CONV_WIDTH
OFF_S5 = OFF_CX + CONV_WIDTH
OFF_G = OFF_S5 + S5_WIDTH
IN_COLS = OFF_G + N_BRANCH * D_MODEL

LANES = 128
SUBLANES = 8
HEAD_PAD = LANES

P_QKV = 0
P_CB = P_QKV + Q_LORA + KV_LORA
P_CC = P_CB + CONV_WIDTH
P_CX = P_CC + CONV_WIDTH
P_S5 = P_CX + CONV_WIDTH
P_G = P_S5 + S5_WIDTH
P_PE = P_G + N_BRANCH * D_MODEL
P_COLS = P_PE + LANES

TM = 256
S5_CHUNK = 16
S5_ROW = S5_CHUNK * S5_GROUP
VMEM_LIMIT = 56 * 1024 * 1024


def _cparams(n_axes):
    return pltpu.CompilerParams(dimension_semantics=("arbitrary",) * n_axes,
                                vmem_limit_bytes=VMEM_LIMIT)


def _const_spec(shape):
    nd = len(shape)
    return pl.BlockSpec(shape, lambda *_: (0,) * nd)


def _rms(x, g):
    return x * lax.rsqrt(jnp.mean(x * x, axis=-1, keepdims=True) + EPS) * g


def _sigmoid(x):
    return 1.0 / (1.0 + jnp.exp(-x))


def _dot(a, b):
    return jnp.dot(a, b, preferred_element_type=F32)


def _ada_kernel(c_ref, w_ref, b_ref, o_ref):
    c = c_ref[...]
    s = (c * _sigmoid(c)).astype(BF16)
    o_ref[0] = _dot(s, w_ref[0].astype(BF16)) + b_ref[0]


def _ada_call(c16, ada_w, ada_b):
    depth = ada_w.shape[0]
    n_col = N_MOD * D_MODEL
    bn = n_col // 4
    return pl.pallas_call(
        _ada_kernel,
        grid=(depth, n_col // bn),
        in_specs=[pl.BlockSpec((16, D_MODEL), lambda l, j: (0, 0)),
                  pl.BlockSpec((1, D_MODEL, bn), lambda l, j: (l, 0, j)),
                  pl.BlockSpec((1, 1, bn), lambda l, j: (l, 0, j))],
        out_specs=pl.BlockSpec((1, 16, bn), lambda l, j: (l, 0, j)),
        out_shape=jax.ShapeDtypeStruct((depth, 16, n_col), F32),
        compiler_params=_cparams(2),
        name="ada_mod",
    )(c16, ada_w, ada_b.reshape(depth, 1, n_col))


def _s5prep_kernel(are_ref, aim_ref, ldt_ref, btre_ref, btim_ref, ctre_ref, ctim_ref, cre_ref, cim_ref,
                   xre_ref, xim_ref, kk_ref, cqre_ref, cqim_ref, a16re_ref, a16im_ref):
    are = are_ref[0]
    aim = aim_ref[0]
    dt = jnp.exp(ldt_ref[0])
    mag = jnp.exp(dt * are)
    abre = mag * jnp.cos(dt * aim)
    abim = mag * jnp.sin(dt * aim)
    den = are * are + aim * aim
    nr = abre - 1.0
    ni = abim
    fre = (nr * are + ni * aim) / den
    fim = (ni * are - nr * aim) / den
    k = (lax.broadcasted_iota(jnp.int32, (S5_ROW, 1), 0) // S5_GROUP).astype(F32)

    def power(p):
        m = jnp.exp(p * (dt * are))
        ang = p * (dt * aim)
        return m * jnp.cos(ang), m * jnp.sin(ang)

    pre, pim = power(k)
    btre = btre_ref[0]
    btim = btim_ref[0]
    bbre = fre * btre - fim * btim
    bbim = fre * btim + fim * btre
    xre = pre * bbre - pim * bbim
    xim = pre * bbim + pim * bbre
    xre_ref[0] = xre
    xim_ref[0] = xim
    dn = (((1,), (1,)), ((), ()))
    kk_ref[0] = (lax.dot_general(xre, cre_ref[0], dn, precision=lax.Precision.HIGHEST, preferred_element_type=F32)
                 - lax.dot_general(xim, cim_ref[0], dn, precision=lax.Precision.HIGHEST,
                                   preferred_element_type=F32))
    qre, qim = power(k + 1.0)
    ctre = ctre_ref[0]
    ctim = ctim_ref[0]
    cqre_ref[0] = ctre * qre - ctim * qim
    cqim_ref[0] = -(ctre * qim + ctim * qre)
    e16 = jnp.exp(float(S5_CHUNK) * (dt * are))
    a16re_ref[0] = e16 * jnp.cos(float(S5_CHUNK) * (dt * aim))
    a16im_ref[0] = e16 * jnp.sin(float(S5_CHUNK) * (dt * aim))


def _s5_operators(a_re, a_im, log_dt, b_re, b_im, c_re, c_im):
    depth = a_re.shape[0]
    n = depth * 2 * S5_GROUPS
    row = lambda t: t.reshape(n, 1, S5_STATE)
    ldt = jnp.broadcast_to(log_dt.reshape(n, 1, 1), (n, 1, S5_STATE))

    def tile_b(t):
        t = jnp.swapaxes(t.reshape(n, S5_STATE, S5_GROUP), 1, 2)
        return jnp.broadcast_to(t[:, None], (n, S5_CHUNK, S5_GROUP, S5_STATE)).reshape(n, S5_ROW, S5_STATE)

    def tile_c(t):
        t = t.reshape(n, S5_GROUP, S5_STATE)
        return jnp.broadcast_to(t[:, None], (n, S5_CHUNK, S5_GROUP, S5_STATE)).reshape(n, S5_ROW, S5_STATE)

    big = pl.BlockSpec((1, S5_ROW, S5_STATE), lambda m: (m, 0, 0))
    vec = pl.BlockSpec((1, 1, S5_STATE), lambda m: (m, 0, 0))
    craw = pl.BlockSpec((1, S5_GROUP, S5_STATE), lambda m: (m, 0, 0))
    big_s = jax.ShapeDtypeStruct((n, S5_ROW, S5_STATE), F32)
    vec_s = jax.ShapeDtypeStruct((n, 1, S5_STATE), F32)
    xre, xim, kk, cqre, cqim, a16re, a16im = pl.pallas_call(
        _s5prep_kernel,
        grid=(n,),
        in_specs=[vec, vec, vec, big, big, big, big, craw, craw],
        out_specs=[big, big, pl.BlockSpec((1, S5_ROW, S5_GROUP), lambda m: (m, 0, 0)), big, big, vec, vec],
        out_shape=[big_s, big_s, jax.ShapeDtypeStruct((n, S5_ROW, S5_GROUP), F32), big_s, big_s, vec_s, vec_s],
        compiler_params=_cparams(1),
        name="s5_operators",
    )(row(a_re), row(a_im), ldt, tile_b(b_re), tile_b(b_im), tile_c(c_re), tile_c(c_im),
      c_re.reshape(n, S5_GROUP, S5_STATE), c_im.reshape(n, S5_GROUP, S5_STATE))

    g = S5_GROUPS
    ck = S5_CHUNK
    x5 = lambda t: t.reshape(depth, 2, g, ck, S5_GROUP, S5_STATE)
    xre, xim, cqre, cqim = x5(xre), x5(xim), x5(cqre), x5(cqim)
    kk = kk.reshape(depth, 2, g, ck, S5_GROUP, S5_GROUP)

    def b_mat(x):
        fwd = jnp.flip(x[:, 0], axis=2).reshape(depth, g, S5_ROW, S5_STATE)
        bwd = x[:, 1].reshape(depth, g, S5_ROW, S5_STATE)
        return jnp.concatenate([fwd, bwd], axis=-1).astype(BF16)

    def c_mat(c):
        fwd = jnp.transpose(c[:, 0], (0, 1, 4, 2, 3)).reshape(depth, g, S5_STATE, S5_ROW)
        bwd = jnp.transpose(jnp.flip(c[:, 1], axis=2), (0, 1, 4, 2, 3)).reshape(depth, g, S5_STATE, S5_ROW)
        return jnp.concatenate([fwd, bwd], axis=2).astype(BF16)

    i_idx = jnp.arange(ck)[:, None]
    j_idx = jnp.arange(ck)[None, :]

    def toeplitz(kd, lag, keep):
        t = kd[:, :, jnp.clip(lag, 0, ck - 1)]
        t = jnp.where(keep[None, None, :, :, None, None], t, 0.0)
        return jnp.transpose(t, (0, 1, 2, 4, 3, 5)).reshape(depth, g, S5_ROW, S5_ROW)

    tm = (toeplitz(kk[:, 0], j_idx - i_idx, j_idx >= i_idx)
          + toeplitz(kk[:, 1], i_idx - j_idx, i_idx >= j_idx)).astype(BF16)
    a16 = lambda t: jnp.concatenate([t.reshape(depth, 2, g, 1, S5_STATE)[:, 0],
                                     t.reshape(depth, 2, g, 1, S5_STATE)[:, 1]], axis=-1)
    return dict(tm=tm, bre=b_mat(xre), bim=b_mat(xim), cre=c_mat(cqre), cim=c_mat(cqim),
                are=a16(a16re), aim=a16(a16im))


def _s5_kernel(u_ref, tm_ref, bre_ref, bim_ref, cre_ref, cim_ref, are_ref, aim_ref, y_ref,
               lre, lim, sre, sim, *, n_chunks, n_ctx_chunks):
    u = u_ref[0]
    lre[...] = _dot(u, bre_ref[0])
    lim[...] = _dot(u, bim_ref[0])
    ar = are_ref[0]
    ai = aim_ref[0]
    nb = u_ref.shape[1] // n_chunks
    fwd_lane = lax.broadcasted_iota(jnp.int32, (nb, 2 * S5_STATE), 1) < S5_STATE

    def body(kstep, carry):
        st_re, st_im = carry
        rb = jnp.where(kstep < n_ctx_chunks, n_ctx_chunks - 1 - kstep, n_chunks - 1 + n_ctx_chunks - kstep)
        of = pl.multiple_of(kstep * nb, SUBLANES)
        ob = pl.multiple_of(rb * nb, SUBLANES)
        sre[pl.ds(of, nb), 0:S5_STATE] = st_re[:, 0:S5_STATE]
        sre[pl.ds(ob, nb), S5_STATE:2 * S5_STATE] = st_re[:, S5_STATE:2 * S5_STATE]
        sim[pl.ds(of, nb), 0:S5_STATE] = st_im[:, 0:S5_STATE]
        sim[pl.ds(ob, nb), S5_STATE:2 * S5_STATE] = st_im[:, S5_STATE:2 * S5_STATE]
        loc_re = jnp.where(fwd_lane, lre[pl.ds(of, nb), :], lre[pl.ds(ob, nb), :])
        loc_im = jnp.where(fwd_lane, lim[pl.ds(of, nb), :], lim[pl.ds(ob, nb), :])
        return ar * st_re - ai * st_im + loc_re, ar * st_im + ai * st_re + loc_im

    zero = jnp.zeros((nb, 2 * S5_STATE), F32)
    lax.fori_loop(0, n_chunks, body, (zero, zero))
    y_ref[0] = (_dot(u, tm_ref[0]) + _dot(sre[...].astype(BF16), cre_ref[0])
                + _dot(sim[...].astype(BF16), cim_ref[0]))


def _s5_call(u_g, ops, layer, n_chunks, n_ctx_chunks):
    g, rows, _ = u_g.shape
    mat = lambda r, c: pl.BlockSpec((None, 1, r, c), lambda m: (layer, m, 0, 0))
    return pl.pallas_call(
        functools.partial(_s5_kernel, n_chunks=n_chunks, n_ctx_chunks=n_ctx_chunks),
        grid=(g,),
        in_specs=[pl.BlockSpec((1, rows, S5_ROW), lambda m: (m, 0, 0)),
                  mat(S5_ROW, S5_ROW), mat(S5_ROW, 2 * S5_STATE), mat(S5_ROW, 2 * S5_STATE),
                  mat(2 * S5_STATE, S5_ROW), mat(2 * S5_STATE, S5_ROW),
                  mat(1, 2 * S5_STATE), mat(1, 2 * S5_STATE)],
        out_specs=pl.BlockSpec((1, rows, S5_ROW), lambda m: (m, 0, 0)),
        out_shape=jax.ShapeDtypeStruct((g, rows, S5_ROW), F32),
        scratch_shapes=[pltpu.VMEM((rows, 2 * S5_STATE), F32)] * 4,
        compiler_params=_cparams(1),
        name="s5_scan",
    )(u_g, ops["tm"], ops["bre"], ops["bim"], ops["cre"], ops["cim"], ops["are"], ops["aim"])


def _inproj_kernel(x_ref, mod_ref, g_ref, w_ref, qn_ref, kvn_ref, wq_ref, wkv_ref, e_ref, rope_ref,
                   q_out, k_out, v_out, cb_out, uc_out, s5_out, gate_out):
    mod = mod_ref[0]
    xn = _rms(x_ref[...], g_ref[...]) * (1.0 + mod[:, D_MODEL:]) + mod[:, :D_MODEL]
    xn = xn.astype(BF16)

    def proj(a, n):
        return _dot(xn, w_ref[:, a:a + n])

    zqkv = proj(P_QKV, Q_LORA + KV_LORA)
    rope = rope_ref[...]
    cos_q = rope[:, 0:LANES]
    sin_q = rope[:, LANES:2 * LANES]
    tab_k = rope[:, 2 * LANES:3 * LANES]
    qn = _rms(zqkv[:, :Q_LORA], qn_ref[...]).astype(BF16)
    q12 = _dot(qn, wq_ref[...])
    nq = N_HEADS * HEAD_PAD
    for h in range(N_HEADS):
        a = h * HEAD_PAD
        qh = q12[:, a:a + HEAD_PAD] * cos_q + q12[:, nq + a:nq + a + HEAD_PAD] * sin_q
        q_out[:, a:a + HEAD_PAD] = (qh * ATTN_SCALE).astype(BF16)
    kvn = _rms(zqkv[:, Q_LORA:], kvn_ref[...]).astype(BF16)
    kv = _dot(kvn, wkv_ref[...])
    zpe = proj(P_PE, LANES)
    kpe = _dot((zpe * tab_k).astype(BF16), e_ref[...])
    k_out[...] = (kv[:, :nq] + kpe).astype(BF16)
    v_out[...] = kv[:, nq:].astype(BF16)
    cb_out[...] = proj(P_CB, CONV_WIDTH)
    uc_out[...] = proj(P_CC, CONV_WIDTH) * proj(P_CX, CONV_WIDTH)
    s5_out[...] = proj(P_S5, S5_WIDTH)
    gate_out[...] = _sigmoid(proj(P_G, N_BRANCH * D_MODEL))


def _inproj_call(xs, mods, lw, rope, e_mat, nb, tps):
    t = xs.shape[0]
    rowb = lambda n: pl.BlockSpec((TM, n), lambda b, i: (b * tps + i, 0))
    nq = N_HEADS * HEAD_PAD
    nv = N_HEADS * V_DIM
    outs = [(nq, BF16), (nq, BF16), (nv, BF16), (CONV_WIDTH, F32), (CONV_WIDTH, F32), (S5_WIDTH, F32),
            (N_BRANCH * D_MODEL, F32)]
    return pl.pallas_call(
        _inproj_kernel,
        grid=(nb, tps),
        in_specs=[rowb(D_MODEL),
                  pl.BlockSpec((1, 1, 2 * D_MODEL), lambda b, i: (jnp.where(i == 0, nb, b), 0, 0)),
                  _const_spec((1, D_MODEL)),
                  _const_spec((D_MODEL, P_COLS)),
                  _const_spec((1, Q_LORA)), _const_spec((1, KV_LORA)),
                  _const_spec((Q_LORA, 2 * nq)), _const_spec((KV_LORA, nq + nv)),
                  _const_spec((LANES, nq)),
                  pl.BlockSpec((TM, 3 * LANES), lambda b, i: (i, 0))],
        out_specs=[rowb(n) for n, _ in outs],
        out_shape=[jax.ShapeDtypeStruct((t, n), dt) for n, dt in outs],
        compiler_params=_cparams(2),
        name="in_projection",
    )(xs, mods, lw["norm_mix"], lw["w_in"], lw["q_norm"], lw["kv_norm"], lw["wq"], lw["wkv"], e_mat, rope)


def _attn_kernel(q_ref, k_ref, v_ref, o_ref, *, first_tile, n_ctx):
    i = pl.program_id(1) + first_tile
    nk = k_ref.shape[0]
    col = lax.broadcasted_iota(jnp.int32, (TM, nk), 1)
    visible = col < jnp.where(i == 0, n_ctx, nk)
    outs = []
    for h in range(N_HEADS):
        q = q_ref[:, h * HEAD_PAD:(h + 1) * HEAD_PAD]
        k = k_ref[:, h * HEAD_PAD:(h + 1) * HEAD_PAD]
        s = lax.dot_general(q, k, (((1,), (1,)), ((), ())), preferred_element_type=F32)
        s = jnp.where(visible, s, -1e30)
        p = jnp.exp(s - jnp.max(s, axis=-1, keepdims=True))
        l = jnp.sum(p, axis=-1, keepdims=True)
        o = _dot(p.astype(BF16), v_ref[:, h * V_DIM:(h + 1) * V_DIM])
        outs.append(o / l)
    o_ref[...] = jnp.concatenate(outs, axis=-1)


def _attn_call(q, k, v, nb, tps, first_tile, n_ctx):
    t = q.shape[0]
    s = tps * TM
    nq = N_HEADS * HEAD_PAD
    nv = N_HEADS * V_DIM
    return pl.pallas_call(
        functools.partial(_attn_kernel, first_tile=first_tile, n_ctx=n_ctx),
        grid=(nb, tps - first_tile),
        in_specs=[pl.BlockSpec((TM, nq), lambda b, i: (b * tps + i + first_tile, 0)),
                  pl.BlockSpec((s, nq), lambda b, i: (b, 0)),
                  pl.BlockSpec((s, nv), lambda b, i: (b, 0))],
        out_specs=pl.BlockSpec((TM, nv), lambda b, i: (b * tps + i + first_tile, 0)),
        out_shape=jax.ShapeDtypeStruct((t, nv), F32),
        compiler_params=_cparams(2),
        name="attention",
    )(q, k, v)


def _gelu_tanh(x):
    return 0.5 * x * (1.0 + jnp.tanh(math.sqrt(2.0 / math.pi) * (x + 0.044715 * (x * x * x))))


def _merge_kernel(x_ref, mod_ref, o_ref, cb_ref, uc_ref, ucp_ref, ucn_ref, ys_ref, s5u_ref, gate_ref,
                  wo_ref, cw_ref, cwo_ref, d_ref, wglu_ref, wout_ref, out_ref, *, first_tile, tps):
    i = pl.program_id(1) + first_tile
    att = _dot(o_ref[...].astype(BF16), wo_ref[...])
    uc = uc_ref[...]
    prev_row = jnp.where(i >= 2, ucp_ref[SUBLANES - 1:SUBLANES, :], 0.0)
    next_row = jnp.where(jnp.logical_and(i >= 1, i < tps - 1), ucn_ref[0:1, :], 0.0)
    row = lax.broadcasted_iota(jnp.int32, (TM, 1), 0)
    up = jnp.where(row == 0, prev_row, pltpu.roll(uc, 1, axis=0))
    dn = jnp.where(row == TM - 1, next_row, pltpu.roll(uc, TM - 1, axis=0))
    y = up * cw_ref[0:1, :] + uc * cw_ref[1:2, :] + dn * cw_ref[2:3, :]
    conv = _dot((cb_ref[...] * y).astype(BF16), cwo_ref[...])
    ys = ys_ref[...] + d_ref[...] * s5u_ref[...]
    z = _dot(_gelu_tanh(ys).astype(BF16), wglu_ref[...])
    s5o = z[:, :D_MODEL] * _sigmoid(z[:, D_MODEL:])
    g = gate_ref[...]
    merged = g[:, :D_MODEL] * att + g[:, D_MODEL:2 * D_MODEL] * conv + g[:, 2 * D_MODEL:] * s5o
    out_ref[...] = x_ref[...] + mod_ref[0] * _dot(merged.astype(BF16), wout_ref[...])


def _merge_call(xs, mods, o, cb, uc, ys, s5u, gate, lw, nb, tps, first_tile):
    t = xs.shape[0]
    blk = lambda b, i: b * tps + i + first_tile
    rowb = lambda n: pl.BlockSpec((TM, n), lambda b, i: (blk(b, i), 0))
    per8 = TM // SUBLANES
    last8 = t // SUBLANES - 1
    return pl.pallas_call(
        functools.partial(_merge_kernel, first_tile=first_tile, tps=tps),
        grid=(nb, tps - first_tile),
        in_specs=[rowb(D_MODEL),
                  pl.BlockSpec((1, 1, D_MODEL), lambda b, i: (jnp.where(i + first_tile == 0, nb, b), 0, 2)),
                  rowb(N_HEADS * V_DIM), rowb(CONV_WIDTH), rowb(CONV_WIDTH),
                  pl.BlockSpec((SUBLANES, CONV_WIDTH), lambda b, i: (jnp.maximum(blk(b, i) * per8 - 1, 0), 0)),
                  pl.BlockSpec((SUBLANES, CONV_WIDTH),
                               lambda b, i: (jnp.minimum((blk(b, i) + 1) * per8, last8), 0)),
                  rowb(S5_WIDTH), rowb(S5_WIDTH), rowb(N_BRANCH * D_MODEL),
                  _const_spec((N_HEADS * V_DIM, D_MODEL)), _const_spec((CONV_K, CONV_WIDTH)),
                  _const_spec((CONV_WIDTH, D_MODEL)), _const_spec((1, S5_WIDTH)),
                  _const_spec((S5_WIDTH, 2 * D_MODEL)), _const_spec((D_MODEL, D_MODEL))],
        out_specs=rowb(D_MODEL),
        out_shape=jax.ShapeDtypeStruct((t, D_MODEL), F32),
        compiler_params=_cparams(2),
        name="branch_merge",
    )(xs, mods, o, cb, uc, uc, uc, ys, s5u, gate,
      lw["w_o"], lw["conv_w"], lw["conv_w_out"], lw["s5_d"], lw["w_glu"], lw["w_out"])


def _mlp_kernel(x_ref, mod_ref, g_ref, w1_ref, w2_ref, gf_ref, out_ref, *, final):
    x = x_ref[...]
    mod = mod_ref[0]
    h = (_rms(x, g_ref[...]) * (1.0 + mod[:, D_MODEL:2 * D_MODEL]) + mod[:, :D_MODEL]).astype(BF16)
    acc = jnp.zeros((TM, D_MODEL), F32)
    for c in range(D_FF // D_MODEL):
        a = jnp.maximum(_dot(h, w1_ref[:, c * D_MODEL:(c + 1) * D_MODEL]), 0.0)
        acc = acc + _dot((a * a).astype(BF16), w2_ref[c * D_MODEL:(c + 1) * D_MODEL, :])
    y = x + mod[:, 2 * D_MODEL:] * acc
    if final:
        y = _rms(y, gf_ref[...])
    out_ref[...] = y


def _mlp_call(x1, mods, lw, norm_final, nb, tps, first_tile, final):
    n_tiles = tps - first_tile
    return pl.pallas_call(
        functools.partial(_mlp_kernel, final=final),
        grid=(nb, n_tiles),
        in_specs=[pl.BlockSpec((TM, D_MODEL), lambda b, i: (b * tps + i + first_tile, 0)),
                  pl.BlockSpec((1, 1, 3 * D_MODEL), lambda b, i: (jnp.where(i + first_tile == 0, nb, b), 0, 1)),
                  _const_spec((1, D_MODEL)),
                  _const_spec((D_MODEL, D_FF)), _const_spec((D_FF, D_MODEL)),
                  _const_spec((1, D_MODEL))],
        out_specs=pl.BlockSpec((TM, D_MODEL), lambda b, i: (b * n_tiles + i, 0)),
        out_shape=jax.ShapeDtypeStruct((nb * n_tiles * TM, D_MODEL), F32),
        compiler_params=_cparams(2),
        name="mlp",
    )(x1, mods, lw["norm_mlp"], lw["w1"], lw["w2"], norm_final)


def _rot_half(w):
    wr = w.reshape(w.shape[:-1] + (2, 2, QK_ROPE // 4))
    return jnp.concatenate([-wr[..., 1:, :], wr[..., :1, :]], axis=-2).reshape(w.shape)


def _layer_weights(i, w_in, norm_mix, q_norm, w_uq, kv_norm, w_ukv, w_o, conv_w, conv_w_out, s5_d, w_glu,
                   w_out, norm_mlp, w1, w2):
    wi = w_in[i]
    pe = wi[:, OFF_PE:OFF_CB]
    w_in_p = jnp.concatenate(
        [wi[:, OFF_Q:OFF_PE], wi[:, OFF_CB:OFF_G], wi[:, OFF_G:], pe, _rot_half(pe),
         jnp.zeros((D_MODEL, LANES - 2 * QK_ROPE), F32)], axis=1).astype(BF16)
    uq = w_uq[i].reshape(Q_LORA, N_HEADS, QK_NOPE + QK_ROPE)
    zpad = jnp.zeros((Q_LORA, N_HEADS, HEAD_PAD - QK_NOPE - QK_ROPE), F32)
    wq1 = jnp.concatenate([uq, zpad], axis=-1).reshape(Q_LORA, N_HEADS * HEAD_PAD)
    wq2 = jnp.concatenate([jnp.zeros((Q_LORA, N_HEADS, QK_NOPE), F32), _rot_half(uq[..., QK_NOPE:]), zpad],
                          axis=-1).reshape(Q_LORA, N_HEADS * HEAD_PAD)
    ukv = w_ukv[i].reshape(KV_LORA, N_HEADS, QK_NOPE + V_DIM)
    wk = jnp.concatenate([ukv[..., :QK_NOPE], jnp.zeros((KV_LORA, N_HEADS, HEAD_PAD - QK_NOPE), F32)],
                         axis=-1).reshape(KV_LORA, N_HEADS * HEAD_PAD)
    wv = ukv[..., QK_NOPE:].reshape(KV_LORA, N_HEADS * V_DIM)
    return dict(
        w_in=w_in_p, norm_mix=norm_mix[i].reshape(1, D_MODEL),
        q_norm=q_norm[i].reshape(1, Q_LORA), kv_norm=kv_norm[i].reshape(1, KV_LORA),
        wq=jnp.concatenate([wq1, wq2], axis=1).astype(BF16),
        wkv=jnp.concatenate([wk, wv], axis=1).astype(BF16),
        w_o=w_o[i].astype(BF16), conv_w=conv_w[i], conv_w_out=conv_w_out[i].astype(BF16),
        s5_d=s5_d[i].reshape(1, S5_WIDTH), w_glu=w_glu[i].astype(BF16), w_out=w_out[i].astype(BF16),
        norm_mlp=norm_mlp[i].reshape(1, D_MODEL), w1=w1[i].astype(BF16), w2=w2[i].astype(BF16))


def _rope_tables(n_ctx, n_tokens):
    rows = n_tokens // GRID_W
    pos = jnp.stack([jnp.repeat(jnp.arange(rows), GRID_W), jnp.tile(jnp.arange(GRID_W), rows)], -1).astype(F32)
    n_freq = QK_ROPE // 4
    inv = ROPE_THETA ** (-jnp.arange(n_freq, dtype=F32) / n_freq)
    ang = pos[:, :, None, None] * inv[None, None, None, :]
    ang = jnp.broadcast_to(ang, (n_tokens, 2, 2, n_freq)).reshape(n_tokens, QK_ROPE)
    cos = jnp.concatenate([jnp.ones((n_ctx, QK_ROPE), F32), jnp.cos(ang)], axis=0)
    sin = jnp.concatenate([jnp.zeros((n_ctx, QK_ROPE), F32), jnp.sin(ang)], axis=0)
    s = n_ctx + n_tokens
    one = jnp.ones((s, QK_NOPE), F32)
    z = lambda n: jnp.zeros((s, n), F32)
    pad = HEAD_PAD - QK_NOPE - QK_ROPE
    return jnp.concatenate([one, cos, z(pad), z(QK_NOPE), sin, z(pad), cos, sin, z(LANES - 2 * QK_ROPE)], axis=1)


def _pe_spread():
    j = jnp.arange(QK_ROPE)
    e = jnp.zeros((LANES, N_HEADS * HEAD_PAD), F32)
    for h in range(N_HEADS):
        e = e.at[j, h * HEAD_PAD + QK_NOPE + j].set(1.0)
        e = e.at[QK_ROPE + j, h * HEAD_PAD + QK_NOPE + j].set(1.0)
    return e.astype(BF16)


def kernel(x, c, ctx, c_ctx, ada_w, ada_b, norm_mix, w_in, mla_q_norm, mla_w_uq, mla_kv_norm, mla_w_ukv, mla_w_o, conv_w, conv_w_out, s5_a_re, s5_a_im, s5_log_dt, s5_b_re, s5_b_im, s5_c_re, s5_c_im, s5_d, s5_w_glu, w_out, norm_mlp, mlp_w1, mlp_w2, norm_final):
    nb, n_lat, _ = x.shape
    n_ctx = ctx.shape[1]
    depth = ada_w.shape[0]
    assert n_ctx == TM and n_lat % TM == 0 and n_lat % GRID_W == 0 and nb == SUBLANES
    s = n_ctx + n_lat
    tps = s // TM
    t = nb * s
    n_chunks = s // S5_CHUNK
    n_ctx_chunks = n_ctx // S5_CHUNK

    c16 = jnp.zeros((16, D_MODEL), F32).at[:nb].set(c).at[nb].set(c_ctx)
    mods = _ada_call(c16, ada_w, ada_b)[:, :nb + 1].reshape(depth, nb + 1, 1, N_MOD * D_MODEL)
    ops = _s5_operators(s5_a_re, s5_a_im, s5_log_dt, s5_b_re, s5_b_im, s5_c_re, s5_c_im)
    rope = _rope_tables(n_ctx, n_lat)
    e_mat = _pe_spread()
    xs = jnp.concatenate([ctx, x], axis=1).reshape(t, D_MODEL)
    gf = norm_final.reshape(1, D_MODEL)

    for i in range(depth):
        last = i == depth - 1
        ft = 1 if last else 0
        lw = _layer_weights(i, w_in, norm_mix, mla_q_norm, mla_w_uq, mla_kv_norm, mla_w_ukv, mla_w_o, conv_w,
                            conv_w_out, s5_d, s5_w_glu, w_out, norm_mlp, mlp_w1, mlp_w2)
        q, k, v, cb, uc, s5u, gate = _inproj_call(xs, mods[i], lw, rope, e_mat, nb, tps)
        o = _attn_call(q, k, v, nb, tps, ft, n_ctx)
        u_g = s5u.reshape(nb, n_chunks, S5_CHUNK, S5_GROUPS, S5_GROUP).transpose(3, 1, 0, 2, 4)
        u_g = u_g.reshape(S5_GROUPS, n_chunks * nb, S5_ROW).astype(BF16)
        y_g = _s5_call(u_g, ops, i, n_chunks, n_ctx_chunks)
        ys = y_g.reshape(S5_GROUPS, n_chunks, nb, S5_CHUNK, S5_GROUP).transpose(2, 1, 3, 0, 4).reshape(t, S5_WIDTH)
        x1 = _merge_call(xs, mods[i], o, cb, uc, ys, s5u, gate, lw, nb, tps, ft)
        xs = _mlp_call(x1, mods[i], lw, gf, nb, tps, ft, last)
    return xs.reshape(nb, n_lat, D_MODEL)
```

```python
import functools
import math

import jax
import jax.numpy as jnp
from jax import lax
from jax.experimental import pallas as pl
from jax.experimental.pallas import tpu as pltpu

F32 = jnp.float32
BF16 = jnp.bfloat16

D_MODEL = 1024
GRID_W = 64
N_HEADS = 8
QK_NOPE = 64
QK_ROPE = 32
V_DIM = 64
Q_LORA = 256
KV_LORA = 256
ROPE_THETA = 10000.0
ATTN_SCALE = 1.0 / math.sqrt(QK_NOPE + QK_ROPE)
Q_SCALE = ATTN_SCALE * math.log2(math.e)
CONV_WIDTH = 512
CONV_K = 3
S5_WIDTH = 512
S5_GROUP = 16
S5_GROUPS = S5_WIDTH // S5_GROUP
S5_STATE = 64
N_BRANCH = 3
D_FF = 4 * D_MODEL
EPS = 1e-6
N_MOD = 6

OFF_Q = 0
OFF_KV = OFF_Q + Q_LORA
OFF_PE = OFF_KV + KV_LORA
OFF_CB = OFF_PE + QK_ROPE
OFF_CC = OFF_CB + CONV_WIDTH
OFF_CX = OFF_CC + CONV_WIDTH
OFF_S5 = OFF_CX + CONV_WIDTH
OFF_G = OFF_S5 + S5_WIDTH
IN_COLS = OFF_G + N_BRANCH * D_MODEL

LANES = 128
SUBLANES = 8
HEAD_PAD = LANES

P_QKV = 0
P_CB = P_QKV + Q_LORA + KV_LORA
P_CC = P_CB + CONV_WIDTH
P_CX = P_CC + CONV_WIDTH
P_S5 = P_CX + CONV_WIDTH
P_G = P_S5 + S5_WIDTH
P_PE = P_G + N_BRANCH * D_MODEL
P_COLS = P_PE + LANES

TM = 256
S5_CHUNK = 16
S5_ROW = S5_CHUNK * S5_GROUP
GROUPS_PER_SLAB = LANES // S5_GROUP
S5_SLABS = S5_WIDTH // LANES
VMEM_LIMIT = 56 * 1024 * 1024


def _cparams(n_axes):
    return pltpu.CompilerParams(dimension_semantics=("arbitrary",) * n_axes,
                                vmem_limit_bytes=VMEM_LIMIT)


def _const_spec(shape):
    nd = len(shape)
    return pl.BlockSpec(shape, lambda *_: (0,) * nd)


def _rms(x, g):
    return x * lax.rsqrt(jnp.mean(x * x, axis=-1, keepdims=True) + EPS) * g


def _sigmoid(x):
    return 1.0 / (1.0 + jnp.exp(-x))


def _dot(a, b):
    return jnp.dot(a, b, preferred_element_type=F32)


def _ada_kernel(c_ref, w_ref, b_ref, o_ref):
    c = c_ref[...]
    s = (c * _sigmoid(c)).astype(BF16)
    o_ref[0] = _dot(s, w_ref[0].astype(BF16)) + b_ref[0]


def _ada_call(c16, ada_w, ada_b):
    depth = ada_w.shape[0]
    n_col = N_MOD * D_MODEL
    bn = n_col // 4
    return pl.pallas_call(
        _ada_kernel,
        grid=(depth, n_col // bn),
        in_specs=[pl.BlockSpec((16, D_MODEL), lambda l, j: (0, 0)),
                  pl.BlockSpec((1, D_MODEL, bn), lambda l, j: (l, 0, j)),
                  pl.BlockSpec((1, 1, bn), lambda l, j: (l, 0, j))],
        out_specs=pl.BlockSpec((1, 16, bn), lambda l, j: (l, 0, j)),
        out_shape=jax.ShapeDtypeStruct((depth, 16, n_col), F32),
        compiler_params=_cparams(2),
        name="ada_mod",
    )(c16, ada_w, ada_b.reshape(depth, 1, n_col))


def _s5prep_kernel(are_ref, aim_ref, ldt_ref, btre_ref, btim_ref, ctre_ref, ctim_ref, cre_ref, cim_ref,
                   xre_ref, xim_ref, kk_ref, cqre_ref, cqim_ref, a16re_ref, a16im_ref):
    are = are_ref[0]
    aim = aim_ref[0]
    dt = jnp.exp(ldt_ref[0])
    mag = jnp.exp(dt * are)
    abre = mag * jnp.cos(dt * aim)
    abim = mag * jnp.sin(dt * aim)
    den = are * are + aim * aim
    nr = abre - 1.0
    ni = abim
    fre = (nr * are + ni * aim) / den
    fim = (ni * are - nr * aim) / den
    k = (lax.broadcasted_iota(jnp.int32, (S5_ROW, 1), 0) // S5_GROUP).astype(F32)

    def power(p):
        m = jnp.exp(p * (dt * are))
        ang = p * (dt * aim)
        return m * jnp.cos(ang), m * jnp.sin(ang)

    pre, pim = power(k)
    btre = btre_ref[0]
    btim = btim_ref[0]
    bbre = fre * btre - fim * btim
    bbim = fre * btim + fim * btre
    xre = pre * bbre - pim * bbim
    xim = pre * bbim + pim * bbre
    xre_ref[0] = xre
    xim_ref[0] = xim
    dn = (((1,), (1,)), ((), ()))
    kk_ref[0] = (lax.dot_general(xre, cre_ref[0], dn, precision=lax.Precision.HIGHEST, preferred_element_type=F32)
                 - lax.dot_general(xim, cim_ref[0], dn, precision=lax.Precision.HIGHEST,
                                   preferred_element_type=F32))
    qre, qim = power(k + 1.0)
    ctre = ctre_ref[0]
    ctim = ctim_ref[0]
    cqre_ref[0] = ctre * qre - ctim * qim
    cqim_ref[0] = -(ctre * qim + ctim * qre)
    e16 = jnp.exp(float(S5_CHUNK) * (dt * are))
    a16re_ref[0] = e16 * jnp.cos(float(S5_CHUNK) * (dt * aim))
    a16im_ref[0] = e16 * jnp.sin(float(S5_CHUNK) * (dt * aim))


def _s5_operators(a_re, a_im, log_dt, b_re, b_im, c_re, c_im):
    depth = a_re.shape[0]
    n = depth * 2 * S5_GROUPS
    row = lambda t: t.reshape(n, 1, S5_STATE)
    ldt = jnp.broadcast_to(log_dt.reshape(n, 1, 1), (n, 1, S5_STATE))

    def tile_b(t):
        t = jnp.swapaxes(t.reshape(n, S5_STATE, S5_GROUP), 1, 2)
        return jnp.broadcast_to(t[:, None], (n, S5_CHUNK, S5_GROUP, S5_STATE)).reshape(n, S5_ROW, S5_STATE)

    def tile_c(t):
        t = t.reshape(n, S5_GROUP, S5_STATE)
        return jnp.broadcast_to(t[:, None], (n, S5_CHUNK, S5_GROUP, S5_STATE)).reshape(n, S5_ROW, S5_STATE)

    big = pl.BlockSpec((1, S5_ROW, S5_STATE), lambda m: (m, 0, 0))
    vec = pl.BlockSpec((1, 1, S5_STATE), lambda m: (m, 0, 0))
    craw = pl.BlockSpec((1, S5_GROUP, S5_STATE), lambda m: (m, 0, 0))
    big_s = jax.ShapeDtypeStruct((n, S5_ROW, S5_STATE), F32)
    vec_s = jax.ShapeDtypeStruct((n, 1, S5_STATE), F32)
    xre, xim, kk, cqre, cqim, a16re, a16im = pl.pallas_call(
        _s5prep_kernel,
        grid=(n,),
        in_specs=[vec, vec, vec, big, big, big, big, craw, craw],
        out_specs=[big, big, pl.BlockSpec((1, S5_ROW, S5_GROUP), lambda m: (m, 0, 0)), big, big, vec, vec],
        out_shape=[big_s, big_s, jax.ShapeDtypeStruct((n, S5_ROW, S5_GROUP), F32), big_s, big_s, vec_s, vec_s],
        compiler_params=_cparams(1),
        name="s5_operators",
    )(row(a_re), row(a_im), ldt, tile_b(b_re), tile_b(b_im), tile_c(c_re), tile_c(c_im),
      c_re.reshape(n, S5_GROUP, S5_STATE), c_im.reshape(n, S5_GROUP, S5_STATE))

    ns, gl, ck = S5_SLABS, GROUPS_PER_SLAB, S5_CHUNK
    x7 = lambda t: t.reshape(depth, 2, ns, gl, ck, S5_GROUP, S5_STATE)
    xre, xim, cqre, cqim = x7(xre), x7(xim), x7(cqre), x7(cqim)
    kk = kk.reshape(depth, 2, ns, gl, ck, S5_GROUP, S5_GROUP)
    pos = jnp.arange(ck)[None, :]
    gi = jnp.arange(gl)
    tau = SUBLANES * (pos // SUBLANES) + (pos % SUBLANES - gi[:, None]) % SUBLANES
    g2 = gi[:, None]
    g3 = gi[:, None, None]

    def b_mat(x):
        fwd = x[:, 0][:, :, g2, ck - 1 - tau].reshape(depth, ns, gl, S5_ROW, S5_STATE)
        bwd = x[:, 1][:, :, g2, tau].reshape(depth, ns, gl, S5_ROW, S5_STATE)
        return jnp.concatenate([fwd, bwd], axis=-1)

    def c_mat(c):
        fwd = jnp.transpose(c[:, 0][:, :, g2, tau], (0, 1, 2, 5, 3, 4)).reshape(depth, ns, gl, S5_STATE, S5_ROW)
        bwd = jnp.transpose(c[:, 1][:, :, g2, ck - 1 - tau], (0, 1, 2, 5, 3, 4)).reshape(depth, ns, gl, S5_STATE, S5_ROW)
        return jnp.concatenate([fwd, bwd], axis=3).astype(BF16)

    def toeplitz(kd, lag):
        t = kd[:, :, g3, jnp.clip(lag, 0, ck - 1)]
        t = jnp.where((lag >= 0)[None, None, :, :, :, None, None], t, 0.0)
        return jnp.transpose(t, (0, 1, 2, 3, 5, 4, 6)).reshape(depth, ns, gl, S5_ROW, S5_ROW)

    lag_f = tau[:, None, :] - tau[:, :, None]
    tm = toeplitz(kk[:, 0], lag_f) + toeplitz(kk[:, 1], -lag_f)
    w1 = jnp.concatenate([tm, b_mat(xre), b_mat(xim)], axis=-1).astype(BF16)
    a16 = lambda t: jnp.concatenate([t.reshape(depth, 2, ns, gl, S5_STATE)[:, 0],
                                     t.reshape(depth, 2, ns, gl, S5_STATE)[:, 1]], axis=-1)
    return dict(w1=w1, cre=c_mat(cqre), cim=c_mat(cqim), are=a16(a16re), aim=a16(a16im))


def _s5_kernel(z_ref, w1_ref, cre_ref, cim_ref, are_ref, aim_ref, d_ref, y_ref,
               yin, lre, lim, sre_f, sre_b, sim_f, sim_b, *, n_chunks, n_ctx_chunks, pitch):
    ng = GROUPS_PER_SLAB
    blk = lax.broadcasted_iota(jnp.int32, (n_chunks, LANES), 1) // S5_GROUP

    def step_rows(j):
        return z_ref[pl.ds(j, n_chunks, stride=S5_CHUNK), :]

    rot = [[step_rows(SUBLANES * h + s) if s == 0
            else pltpu.roll(step_rows(SUBLANES * h + s), S5_GROUP * s, axis=1)
            for s in range(SUBLANES)] for h in range(2)]
    for g in range(ng):
        halves = []
        for h in range(2):
            u = rot[h][(0 - g) % ng]
            for q in range(1, ng):
                u = jnp.where(blk == q, rot[h][(q - g) % ng], u)
            halves.append(u)
        r = _dot(jnp.concatenate(halves, axis=1).astype(BF16), w1_ref[g])
        yin[g] = r[:, :S5_ROW]
        lre[pl.ds(g * pitch, n_chunks), :] = r[:, S5_ROW:S5_ROW + 2 * S5_STATE]
        lim[pl.ds(g * pitch, n_chunks), :] = r[:, S5_ROW + 2 * S5_STATE:]

    ar = are_ref[...]
    ai = aim_ref[...]
    fwd_lane = lax.broadcasted_iota(jnp.int32, (ng, 2 * S5_STATE), 1) < S5_STATE

    def body(kstep, carry):
        st_re, st_im = carry
        rb = jnp.where(kstep < n_ctx_chunks, n_ctx_chunks - 1 - kstep, n_chunks - 1 + n_ctx_chunks - kstep)
        rows_f = pl.ds(kstep, ng, stride=pitch)
        rows_b = pl.ds(rb, ng, stride=pitch)
        sre_f[rows_f, :] = st_re
        sre_b[rows_b, :] = st_re
        sim_f[rows_f, :] = st_im
        sim_b[rows_b, :] = st_im
        loc_re = jnp.where(fwd_lane, lre[rows_f, :], lre[rows_b, :])
        loc_im = jnp.where(fwd_lane, lim[rows_f, :], lim[rows_b, :])
        return ar * st_re - ai * st_im + loc_re, ar * st_im + ai * st_re + loc_im

    zero = jnp.zeros((ng, 2 * S5_STATE), F32)
    lax.fori_loop(0, n_chunks, body, (zero, zero))

    fwd_big = lax.broadcasted_iota(jnp.int32, (n_chunks, 2 * S5_STATE), 1) < S5_STATE
    for g in range(ng):
        rows = pl.ds(g * pitch, n_chunks)
        s_re = jnp.where(fwd_big, sre_f[rows, :], sre_b[rows, :]).astype(BF16)
        s_im = jnp.where(fwd_big, sim_f[rows, :], sim_b[rows, :]).astype(BF16)
        yin[g] = yin[g] + _dot(s_re, cre_ref[g]) + _dot(s_im, cim_ref[g])

    d = d_ref[...]
    for h in range(2):
        for s in range(SUBLANES):
            v = yin[(0 - s) % ng, :, h * LANES:(h + 1) * LANES]
            for q in range(1, ng):
                v = jnp.where(blk == q, yin[(q - s) % ng, :, h * LANES:(h + 1) * LANES], v)
            if s:
                v = pltpu.roll(v, LANES - S5_GROUP * s, axis=1)
            j = SUBLANES * h + s
            y_ref[pl.ds(j, n_chunks, stride=S5_CHUNK), :] = v + d * step_rows(j)


def _s5_call(s5u, ops, s5_d, layer, nb, n_chunks, n_ctx_chunks):
    t = s5u.shape[0]
    s = t // nb
    pitch = -(-n_chunks // SUBLANES) * SUBLANES
    if (pitch // SUBLANES) % 2 == 0:
        pitch += SUBLANES
    gl = GROUPS_PER_SLAB
    mat = lambda r, c: pl.BlockSpec((None, None, gl, r, c), lambda a, b: (layer, a, 0, 0, 0))
    vec = pl.BlockSpec((None, None, gl, 2 * S5_STATE), lambda a, b: (layer, a, 0, 0))
    state = pltpu.VMEM((gl * pitch, 2 * S5_STATE), F32)
    return pl.pallas_call(
        functools.partial(_s5_kernel, n_chunks=n_chunks, n_ctx_chunks=n_ctx_chunks, pitch=pitch),
        grid=(S5_SLABS, nb),
        in_specs=[pl.BlockSpec((s, LANES), lambda a, b: (b, a)),
                  mat(S5_ROW, S5_ROW + 4 * S5_STATE), mat(2 * S5_STATE, S5_ROW), mat(2 * S5_STATE, S5_ROW),
                  vec, vec, pl.BlockSpec((1, LANES), lambda a, b: (0, a))],
        out_specs=pl.BlockSpec((s, LANES), lambda a, b: (b, a)),
        out_shape=jax.ShapeDtypeStruct((t, S5_WIDTH), F32),
        scratch_shapes=[pltpu.VMEM((gl, n_chunks, S5_ROW), F32)] + [state] * 6,
        compiler_params=_cparams(2),
        name="s5_scan",
    )(s5u, ops["w1"], ops["cre"], ops["cim"], ops["are"], ops["aim"], s5_d)


def _inproj_kernel(x_ref, mod_ref, g_ref, w_ref, qn_ref, kvn_ref, wq_ref, wkv_ref, e_ref, rope_ref,
                   q_out, k_out, v_out, cb_out, uc_out, s5_out, gate_out):
    mod = mod_ref[0]
    xn = _rms(x_ref[...], g_ref[...]) * (1.0 + mod[:, D_MODEL:]) + mod[:, :D_MODEL]
    xn = xn.astype(BF16)

    def proj(a, n):
        return _dot(xn, w_ref[:, a:a + n])

    zqkv = proj(P_QKV, Q_LORA + KV_LORA)
    rope = rope_ref[...]
    cos_q = rope[:, 0:LANES]
    sin_q = rope[:, LANES:2 * LANES]
    tab_k = rope[:, 2 * LANES:3 * LANES]
    qn = _rms(zqkv[:, :Q_LORA], qn_ref[...]).astype(BF16)
    q12 = _dot(qn, wq_ref[...])
    nq = N_HEADS * HEAD_PAD
    for h in range(N_HEADS):
        a = h * HEAD_PAD
        qh = q12[:, a:a + HEAD_PAD] * cos_q + q12[:, nq + a:nq + a + HEAD_PAD] * sin_q
        q_out[:, a:a + HEAD_PAD] = (qh * Q_SCALE).astype(BF16)
    kvn = _rms(zqkv[:, Q_LORA:], kvn_ref[...]).astype(BF16)
    kv = _dot(kvn, wkv_ref[...])
    zpe = proj(P_PE, LANES)
    kpe = _dot((zpe * tab_k).astype(BF16), e_ref[...])
    k_out[...] = (kv[:, :nq] + kpe).astype(BF16)
    one_lane = lax.broadcasted_iota(jnp.int32, (1, nq), 1) % HEAD_PAD == V_DIM
    v_out[...] = (kv[:, nq:] + jnp.where(one_lane, 1.0, 0.0)).astype(BF16)
    cb_out[...] = proj(P_CB, CONV_WIDTH)
    uc_out[...] = proj(P_CC, CONV_WIDTH) * proj(P_CX, CONV_WIDTH)
    s5_out[...] = proj(P_S5, S5_WIDTH)
    gate_out[...] = _sigmoid(proj(P_G, N_BRANCH * D_MODEL))


def _inproj_call(xs, mods, lw, rope, e_mat, nb, tps):
    t = xs.shape[0]
    rowb = lambda n: pl.BlockSpec((TM, n), lambda b, i: (b * tps + i, 0))
    nq = N_HEADS * HEAD_PAD
    outs = [(nq, BF16), (nq, BF16), (nq, BF16), (CONV_WIDTH, F32), (CONV_WIDTH, F32), (S5_WIDTH, F32),
            (N_BRANCH * D_MODEL, F32)]
    return pl.pallas_call(
        _inproj_kernel,
        grid=(nb, tps),
        in_specs=[rowb(D_MODEL),
                  pl.BlockSpec((1, 1, 2 * D_MODEL), lambda b, i: (jnp.where(i == 0, nb, b), 0, 0)),
                  _const_spec((1, D_MODEL)),
                  _const_spec((D_MODEL, P_COLS)),
                  _const_spec((1, Q_LORA)), _const_spec((1, KV_LORA)),
                  _const_spec((Q_LORA, 2 * nq)), _const_spec((KV_LORA, 2 * nq)),
                  _const_spec((LANES, nq)),
                  pl.BlockSpec((TM, 3 * LANES), lambda b, i: (i, 0))],
        out_specs=[rowb(n) for n, _ in outs],
        out_shape=[jax.ShapeDtypeStruct((t, n), dt) for n, dt in outs],
        compiler_params=_cparams(2),
        name="in_projection",
    )(xs, mods, lw["norm_mix"], lw["w_in"], lw["q_norm"], lw["kv_norm"], lw["wq"], lw["wkv"], e_mat, rope)


def _attn_head(q, k, v):
    s = lax.dot_general(q, k, (((1,), (1,)), ((), ())), preferred_element_type=F32)
    p = jnp.exp2(s - jnp.max(s, axis=-1, keepdims=True)).astype(BF16)
    oe = _dot(p, v)
    return oe[:, :V_DIM] / oe[:, V_DIM:V_DIM + 1]


def _attn_kernel(q_ref, k_ref, v_ref, o_ref, *, first_tile, n_ctx):
    def run(nk):
        outs = []
        for h in range(N_HEADS):
            sl = slice(h * HEAD_PAD, (h + 1) * HEAD_PAD)
            outs.append(_attn_head(q_ref[:, sl], k_ref[0:nk, sl], v_ref[0:nk, sl]))
        o_ref[...] = jnp.concatenate(outs, axis=-1)

    if first_tile == 0:
        pl.when(pl.program_id(1) == 0)(lambda: run(n_ctx))
        pl.when(pl.program_id(1) != 0)(lambda: run(k_ref.shape[0]))
    else:
        run(k_ref.shape[0])


def _attn_call(q, k, v, nb, tps, first_tile, n_ctx):
    t = q.shape[0]
    s = tps * TM
    nq = N_HEADS * HEAD_PAD
    nv = N_HEADS * V_DIM
    return pl.pallas_call(
        functools.partial(_attn_kernel, first_tile=first_tile, n_ctx=n_ctx),
        grid=(nb, tps - first_tile),
        in_specs=[pl.BlockSpec((TM, nq), lambda b, i: (b * tps + i + first_tile, 0)),
                  pl.BlockSpec((s, nq), lambda b, i: (b, 0)),
                  pl.BlockSpec((s, nq), lambda b, i: (b, 0))],
        out_specs=pl.BlockSpec((TM, nv), lambda b, i: (b * tps + i + first_tile, 0)),
        out_shape=jax.ShapeDtypeStruct((t, nv), F32),
        compiler_params=_cparams(2),
        name="attention",
    )(q, k, v)


def _gelu_tanh(x):
    return 0.5 * x * (1.0 + jnp.tanh(math.sqrt(2.0 / math.pi) * (x + 0.044715 * (x * x * x))))


def _merge_kernel(x_ref, mod_ref, o_ref, cb_ref, uc_ref, ucp_ref, ucn_ref, ys_ref, gate_ref,
                  wo_ref, cw_ref, cwo_ref, wglu_ref, wout_ref, out_ref, *, first_tile, tps):
    i = pl.program_id(1) + first_tile
    att = _dot(o_ref[...].astype(BF16), wo_ref[...])
    uc = uc_ref[...]
    prev_row = jnp.where(i >= 2, ucp_ref[SUBLANES - 1:SUBLANES, :], 0.0)
    next_row = jnp.where(jnp.logical_and(i >= 1, i < tps - 1), ucn_ref[0:1, :], 0.0)
    row = lax.broadcasted_iota(jnp.int32, (TM, 1), 0)
    up = jnp.where(row == 0, prev_row, pltpu.roll(uc, 1, axis=0))
    dn = jnp.where(row == TM - 1, next_row, pltpu.roll(uc, TM - 1, axis=0))
    y = up * cw_ref[0:1, :] + uc * cw_ref[1:2, :] + dn * cw_ref[2:3, :]
    conv = _dot((cb_ref[...] * y).astype(BF16), cwo_ref[...])
    z = _dot(_gelu_tanh(ys_ref[...]).astype(BF16), wglu_ref[...])
    s5o = z[:, :D_MODEL] * _sigmoid(z[:, D_MODEL:])
    g = gate_ref[...]
    merged = g[:, :D_MODEL] * att + g[:, D_MODEL:2 * D_MODEL] * conv + g[:, 2 * D_MODEL:] * s5o
    out_ref[...] = x_ref[...] + mod_ref[0] * _dot(merged.astype(BF16), wout_ref[...])


def _merge_call(xs, mods, o, cb, uc, ys, gate, lw, nb, tps, first_tile):
    t = xs.shape[0]
    blk = lambda b, i: b * tps + i + first_tile
    rowb = lambda n: pl.BlockSpec((TM, n), lambda b, i: (blk(b, i), 0))
    per8 = TM // SUBLANES
    last8 = t // SUBLANES - 1
    return pl.pallas_call(
        functools.partial(_merge_kernel, first_tile=first_tile, tps=tps),
        grid=(nb, tps - first_tile),
        in_specs=[rowb(D_MODEL),
                  pl.BlockSpec((1, 1, D_MODEL), lambda b, i: (jnp.where(i + first_tile == 0, nb, b), 0, 2)),
                  rowb(N_HEADS * V_DIM), rowb(CONV_WIDTH), rowb(CONV_WIDTH),
                  pl.BlockSpec((SUBLANES, CONV_WIDTH), lambda b, i: (jnp.maximum(blk(b, i) * per8 - 1, 0), 0)),
                  pl.BlockSpec((SUBLANES, CONV_WIDTH),
                               lambda b, i: (jnp.minimum((blk(b, i) + 1) * per8, last8), 0)),
                  rowb(S5_WIDTH), rowb(N_BRANCH * D_MODEL),
                  _const_spec((N_HEADS * V_DIM, D_MODEL)), _const_spec((CONV_K, CONV_WIDTH)),
                  _const_spec((CONV_WIDTH, D_MODEL)),
                  _const_spec((S5_WIDTH, 2 * D_MODEL)), _const_spec((D_MODEL, D_MODEL))],
        out_specs=rowb(D_MODEL),
        out_shape=jax.ShapeDtypeStruct((t, D_MODEL), F32),
        compiler_params=_cparams(2),
        name="branch_merge",
    )(xs, mods, o, cb, uc, uc, uc, ys, gate,
      lw["w_o"], lw["conv_w"], lw["conv_w_out"], lw["w_glu"], lw["w_out"])


def _mlp_kernel(x_ref, mod_ref, g_ref, w1_ref, w2_ref, gf_ref, out_ref, *, final):
    x = x_ref[...]
    mod = mod_ref[0]
    h = (_rms(x, g_ref[...]) * (1.0 + mod[:, D_MODEL:2 * D_MODEL]) + mod[:, :D_MODEL]).astype(BF16)
    acc = jnp.zeros((TM, D_MODEL), F32)
    for c in range(D_FF // D_MODEL):
        a = jnp.maximum(_dot(h, w1_ref[:, c * D_MODEL:(c + 1) * D_MODEL]), 0.0)
        acc = acc + _dot((a * a).astype(BF16), w2_ref[c * D_MODEL:(c + 1) * D_MODEL, :])
    y = x + mod[:, 2 * D_MODEL:] * acc
    if final:
        y = _rms(y, gf_ref[...])
    out_ref[...] = y


def _mlp_call(x1, mods, lw, norm_final, nb, tps, first_tile, final):
    n_tiles = tps - first_tile
    return pl.pallas_call(
        functools.partial(_mlp_kernel, final=final),
        grid=(nb, n_tiles),
        in_specs=[pl.BlockSpec((TM, D_MODEL), lambda b, i: (b * tps + i + first_tile, 0)),
                  pl.BlockSpec((1, 1, 3 * D_MODEL), lambda b, i: (jnp.where(i + first_tile == 0, nb, b), 0, 1)),
                  _const_spec((1, D_MODEL)),
                  _const_spec((D_MODEL, D_FF)), _const_spec((D_FF, D_MODEL)),
                  _const_spec((1, D_MODEL))],
        out_specs=pl.BlockSpec((TM, D_MODEL), lambda b, i: (b * n_tiles + i, 0)),
        out_shape=jax.ShapeDtypeStruct((nb * n_tiles * TM, D_MODEL), F32),
        compiler_params=_cparams(2),
        name="mlp",
    )(x1, mods, lw["norm_mlp"], lw["w1"], lw["w2"], norm_final)


def _rot_half(w):
    wr = w.reshape(w.shape[:-1] + (2, 2, QK_ROPE // 4))
    return jnp.concatenate([-wr[..., 1:, :], wr[..., :1, :]], axis=-2).reshape(w.shape)


def _layer_weights(i, w_in, norm_mix, q_norm, w_uq, kv_norm, w_ukv, w_o, conv_w, conv_w_out, s5_d, w_glu,
                   w_out, norm_mlp, w1, w2):
    wi = w_in[i]
    pe = wi[:, OFF_PE:OFF_CB]
    w_in_p = jnp.concatenate(
        [wi[:, OFF_Q:OFF_PE], wi[:, OFF_CB:OFF_G], wi[:, OFF_G:], pe, _rot_half(pe),
         jnp.zeros((D_MODEL, LANES - 2 * QK_ROPE), F32)], axis=1).astype(BF16)
    uq = w_uq[i].reshape(Q_LORA, N_HEADS, QK_NOPE + QK_ROPE)
    zpad = jnp.zeros((Q_LORA, N_HEADS, HEAD_PAD - QK_NOPE - QK_ROPE), F32)
    wq1 = jnp.concatenate([uq, zpad], axis=-1).reshape(Q_LORA, N_HEADS * HEAD_PAD)
    wq2 = jnp.concatenate([jnp.zeros((Q_LORA, N_HEADS, QK_NOPE), F32), _rot_half(uq[..., QK_NOPE:]), zpad],
                          axis=-1).reshape(Q_LORA, N_HEADS * HEAD_PAD)
    ukv = w_ukv[i].reshape(KV_LORA, N_HEADS, QK_NOPE + V_DIM)
    wk = jnp.concatenate([ukv[..., :QK_NOPE], jnp.zeros((KV_LORA, N_HEADS, HEAD_PAD - QK_NOPE), F32)],
                         axis=-1).reshape(KV_LORA, N_HEADS * HEAD_PAD)
    wv = jnp.concatenate([ukv[..., QK_NOPE:], jnp.zeros((KV_LORA, N_HEADS, HEAD_PAD - V_DIM), F32)],
                         axis=-1).reshape(KV_LORA, N_HEADS * HEAD_PAD)
    return dict(
        w_in=w_in_p, norm_mix=norm_mix[i].reshape(1, D_MODEL),
        q_norm=q_norm[i].reshape(1, Q_LORA), kv_norm=kv_norm[i].reshape(1, KV_LORA),
        wq=jnp.concatenate([wq1, wq2], axis=1).astype(BF16),
        wkv=jnp.concatenate([wk, wv], axis=1).astype(BF16),
        w_o=w_o[i].astype(BF16), conv_w=conv_w[i], conv_w_out=conv_w_out[i].astype(BF16),
        s5_d=s5_d[i].reshape(1, S5_WIDTH), w_glu=w_glu[i].astype(BF16), w_out=w_out[i].astype(BF16),
        norm_mlp=norm_mlp[i].reshape(1, D_MODEL), w1=w1[i].astype(BF16), w2=w2[i].astype(BF16))


def _rope_tables(n_ctx, n_tokens):
    rows = n_tokens // GRID_W
    pos = jnp.stack([jnp.repeat(jnp.arange(rows), GRID_W), jnp.tile(jnp.arange(GRID_W), rows)], -1).astype(F32)
    n_freq = QK_ROPE // 4
    inv = ROPE_THETA ** (-jnp.arange(n_freq, dtype=F32) / n_freq)
    ang = pos[:, :, None, None] * inv[None, None, None, :]
    ang = jnp.broadcast_to(ang, (n_tokens, 2, 2, n_freq)).reshape(n_tokens, QK_ROPE)
    cos = jnp.concatenate([jnp.ones((n_ctx, QK_ROPE), F32), jnp.cos(ang)], axis=0)
    sin = jnp.concatenate([jnp.zeros((n_ctx, QK_ROPE), F32), jnp.sin(ang)], axis=0)
    s = n_ctx + n_tokens
    one = jnp.ones((s, QK_NOPE), F32)
    z = lambda n: jnp.zeros((s, n), F32)
    pad = HEAD_PAD - QK_NOPE - QK_ROPE
    return jnp.concatenate([one, cos, z(pad), z(QK_NOPE), sin, z(pad), cos, sin, z(LANES - 2 * QK_ROPE)], axis=1)


def _pe_spread():
    j = jnp.arange(QK_ROPE)
    e = jnp.zeros((LANES, N_HEADS * HEAD_PAD), F32)
    for h in range(N_HEADS):
        e = e.at[j, h * HEAD_PAD + QK_NOPE + j].set(1.0)
        e = e.at[QK_ROPE + j, h * HEAD_PAD + QK_NOPE + j].set(1.0)
    return e.astype(BF16)


def kernel(x, c, ctx, c_ctx, ada_w, ada_b, norm_mix, w_in, mla_q_norm, mla_w_uq, mla_kv_norm, mla_w_ukv, mla_w_o, conv_w, conv_w_out, s5_a_re, s5_a_im, s5_log_dt, s5_b_re, s5_b_im, s5_c_re, s5_c_im, s5_d, s5_w_glu, w_out, norm_mlp, mlp_w1, mlp_w2, norm_final):
    nb, n_lat, _ = x.shape
    n_ctx = ctx.shape[1]
    depth = ada_w.shape[0]
    assert n_ctx == TM and n_lat % TM == 0 and n_lat % GRID_W == 0 and nb == SUBLANES
    s = n_ctx + n_lat
    tps = s // TM
    t = nb * s
    n_chunks = s // S5_CHUNK
    n_ctx_chunks = n_ctx // S5_CHUNK

    c16 = jnp.zeros((16, D_MODEL), F32).at[:nb].set(c).at[nb].set(c_ctx)
    mods = _ada_call(c16, ada_w, ada_b)[:, :nb + 1].reshape(depth, nb + 1, 1, N_MOD * D_MODEL)
    ops = _s5_operators(s5_a_re, s5_a_im, s5_log_dt, s5_b_re, s5_b_im, s5_c_re, s5_c_im)
    rope = _rope_tables(n_ctx, n_lat)
    e_mat = _pe_spread()
    xs = jnp.concatenate([ctx, x], axis=1).reshape(t, D_MODEL)
    gf = norm_final.reshape(1, D_MODEL)

    for i in range(depth):
        last = i == depth - 1
        ft = 1 if last else 0
        lw = _layer_weights(i, w_in, norm_mix, mla_q_norm, mla_w_uq, mla_kv_norm, mla_w_ukv, mla_w_o, conv_w,
                            conv_w_out, s5_d, s5_w_glu, w_out, norm_mlp, mlp_w1, mlp_w2)
        q, k, v, cb, uc, s5u, gate = _inproj_call(xs, mods[i], lw, rope, e_mat, nb, tps)
        o = _attn_call(q, k, v, nb, tps, ft, n_ctx)
        ys = _s5_call(s5u, ops, lw["s5_d"], i, nb, n_chunks, n_ctx_chunks)
        x1 = _merge_call(xs, mods[i], o, cb, uc, ys, gate, lw, nb, tps, ft)
        xs = _mlp_call(x1, mods[i], lw, gf, nb, tps, ft, last)
    return xs.reshape(nb, n_lat, D_MODEL)
```

```python
import functools
import math
from typing import NamedTuple

import jax
import jax.numpy as jnp
from jax import lax
from jax.experimental import pallas as pl
from jax.experimental.pallas import tpu as pltpu

F32 = jnp.float32
BF16 = jnp.bfloat16

D_MODEL = 1024
GRID_W = 64
N_HEADS = 8
QK_NOPE = 64
QK_ROPE = 32
V_DIM = 64
Q_LORA = 256
KV_LORA = 256
ROPE_THETA = 10000.0
ATTN_SCALE = 1.0 / math.sqrt(QK_NOPE + QK_ROPE)
Q_SCALE = ATTN_SCALE * math.log2(math.e)
CONV_WIDTH = 512
CONV_K = 3
S5_WIDTH = 512
S5_GROUP = 16
S5_GROUPS = S5_WIDTH // S5_GROUP
S5_STATE = 64
N_BRANCH = 3
D_FF = 4 * D_MODEL
EPS = 1e-6
N_MOD = 6

OFF_Q = 0
OFF_KV = OFF_Q + Q_LORA
OFF_PE = OFF_KV + KV_LORA
OFF_CB = OFF_PE + QK_ROPE
OFF_CC = OFF_CB + CONV_WIDTH
OFF_CX = OFF_CC + CONV_WIDTH
OFF_S5 = OFF_CX + CONV_WIDTH
OFF_G = OFF_S5 + S5_WIDTH
IN_COLS = OFF_G + N_BRANCH * D_MODEL

LANES = 128
SUBLANES = 8
HEAD_PAD = LANES

P_QKV = 0
P_CB = P_QKV + Q_LORA + KV_LORA
P_CC = P_CB + CONV_WIDTH
P_CX = P_CC + CONV_WIDTH
P_S5 = P_CX + CONV_WIDTH
P_G = P_S5 + S5_WIDTH
P_PE = P_G + N_BRANCH * D_MODEL
P_COLS = P_PE + LANES

TM = 256
TQ = 512
HALO = 16
S5_CHUNK = 16
S5_ROW = S5_CHUNK * S5_GROUP
GROUPS_PER_SLAB = LANES // S5_GROUP
S5_SLABS = S5_WIDTH // LANES
VMEM_LIMIT = 56 * 1024 * 1024


class _TokenSource(NamedTuple):
    ctx: jax.Array
    lat: jax.Array
    ctx_stride: int
    lat_stride: int
    lat_off: int


def _cparams(n_axes):
    return pltpu.CompilerParams(dimension_semantics=("arbitrary",) * n_axes,
                                vmem_limit_bytes=VMEM_LIMIT)


def _const_spec(shape):
    nd = len(shape)
    return pl.BlockSpec(shape, lambda *_: (0,) * nd)


def _rms(x, g):
    return x * lax.rsqrt(jnp.mean(x * x, axis=-1, keepdims=True) + EPS) * g


def _sigmoid(x):
    return 1.0 / (1.0 + jnp.exp(-x))


def _dot(a, b):
    return jnp.dot(a, b, preferred_element_type=F32)


def _ada_kernel(c_ref, w_ref, b_ref, o_ref):
    c = c_ref[...]
    s = (c * _sigmoid(c)).astype(BF16)
    o_ref[0] = _dot(s, w_ref[0].astype(BF16)) + b_ref[0]


def _ada_call(c16, ada_w, ada_b):
    depth = ada_w.shape[0]
    n_col = N_MOD * D_MODEL
    bn = n_col // 4
    return pl.pallas_call(
        _ada_kernel,
        grid=(depth, n_col // bn),
        in_specs=[pl.BlockSpec((16, D_MODEL), lambda l, j: (0, 0)),
                  pl.BlockSpec((1, D_MODEL, bn), lambda l, j: (l, 0, j)),
                  pl.BlockSpec((1, 1, bn), lambda l, j: (l, 0, j))],
        out_specs=pl.BlockSpec((1, 16, bn), lambda l, j: (l, 0, j)),
        out_shape=jax.ShapeDtypeStruct((depth, 16, n_col), F32),
        compiler_params=_cparams(2),
        name="ada_mod",
    )(c16, ada_w, ada_b.reshape(depth, 1, n_col))


def _s5prep_kernel(are_ref, aim_ref, ldt_ref, btre_ref, btim_ref, ctre_ref, ctim_ref, cre_ref, cim_ref,
                   xre_ref, xim_ref, kk_ref, cqre_ref, cqim_ref, a16re_ref, a16im_ref):
    are = are_ref[0]
    aim = aim_ref[0]
    dt = jnp.exp(ldt_ref[0])
    mag = jnp.exp(dt * are)
    abre = mag * jnp.cos(dt * aim)
    abim = mag * jnp.sin(dt * aim)
    den = are * are + aim * aim
    nr = abre - 1.0
    ni = abim
    fre = (nr * are + ni * aim) / den
    fim = (ni * are - nr * aim) / den
    k = (lax.broadcasted_iota(jnp.int32, (S5_ROW, 1), 0) // S5_GROUP).astype(F32)

    def power(p):
        m = jnp.exp(p * (dt * are))
        ang = p * (dt * aim)
        return m * jnp.cos(ang), m * jnp.sin(ang)

    pre, pim = power(k)
    btre = btre_ref[0]
    btim = btim_ref[0]
    bbre = fre * btre - fim * btim
    bbim = fre * btim + fim * btre
    xre = pre * bbre - pim * bbim
    xim = pre * bbim + pim * bbre
    xre_ref[0] = xre
    xim_ref[0] = xim
    dn = (((1,), (1,)), ((), ()))
    kk_ref[0] = (lax.dot_general(xre, cre_ref[0], dn, precision=lax.Precision.HIGHEST, preferred_element_type=F32)
                 - lax.dot_general(xim, cim_ref[0], dn, precision=lax.Precision.HIGHEST,
                                   preferred_element_type=F32))
    qre, qim = power(k + 1.0)
    ctre = ctre_ref[0]
    ctim = ctim_ref[0]
    cqre_ref[0] = ctre * qre - ctim * qim
    cqim_ref[0] = -(ctre * qim + ctim * qre)
    e16 = jnp.exp(float(S5_CHUNK) * (dt * are))
    a16re_ref[0] = e16 * jnp.cos(float(S5_CHUNK) * (dt * aim))
    a16im_ref[0] = e16 * jnp.sin(float(S5_CHUNK) * (dt * aim))


def _s5_operators(a_re, a_im, log_dt, b_re, b_im, c_re, c_im):
    depth = a_re.shape[0]
    n = depth * 2 * S5_GROUPS
    row = lambda t: t.reshape(n, 1, S5_STATE)
    ldt = jnp.broadcast_to(log_dt.reshape(n, 1, 1), (n, 1, S5_STATE))

    def tile_b(t):
        t = jnp.swapaxes(t.reshape(n, S5_STATE, S5_GROUP), 1, 2)
        return jnp.broadcast_to(t[:, None], (n, S5_CHUNK, S5_GROUP, S5_STATE)).reshape(n, S5_ROW, S5_STATE)

    def tile_c(t):
        t = t.reshape(n, S5_GROUP, S5_STATE)
        return jnp.broadcast_to(t[:, None], (n, S5_CHUNK, S5_GROUP, S5_STATE)).reshape(n, S5_ROW, S5_STATE)

    big = pl.BlockSpec((1, S5_ROW, S5_STATE), lambda m: (m, 0, 0))
    vec = pl.BlockSpec((1, 1, S5_STATE), lambda m: (m, 0, 0))
    craw = pl.BlockSpec((1, S5_GROUP, S5_STATE), lambda m: (m, 0, 0))
    big_s = jax.ShapeDtypeStruct((n, S5_ROW, S5_STATE), F32)
    vec_s = jax.ShapeDtypeStruct((n, 1, S5_STATE), F32)
    xre, xim, kk, cqre, cqim, a16re, a16im = pl.pallas_call(
        _s5prep_kernel,
        grid=(n,),
        in_specs=[vec, vec, vec, big, big, big, big, craw, craw],
        out_specs=[big, big, pl.BlockSpec((1, S5_ROW, S5_GROUP), lambda m: (m, 0, 0)), big, big, vec, vec],
        out_shape=[big_s, big_s, jax.ShapeDtypeStruct((n, S5_ROW, S5_GROUP), F32), big_s, big_s, vec_s, vec_s],
        compiler_params=_cparams(1),
        name="s5_operators",
    )(row(a_re), row(a_im), ldt, tile_b(b_re), tile_b(b_im), tile_c(c_re), tile_c(c_im),
      c_re.reshape(n, S5_GROUP, S5_STATE), c_im.reshape(n, S5_GROUP, S5_STATE))

    ns, gl, ck = S5_SLABS, GROUPS_PER_SLAB, S5_CHUNK
    x7 = lambda t: t.reshape(depth, 2, ns, gl, ck, S5_GROUP, S5_STATE)
    xre, xim, cqre, cqim = x7(xre), x7(xim), x7(cqre), x7(cqim)
    kk = kk.reshape(depth, 2, ns, gl, ck, S5_GROUP, S5_GROUP)
    pos = jnp.arange(ck)[None, :]
    gi = jnp.arange(gl)
    tau = SUBLANES * (pos // SUBLANES) + (pos % SUBLANES - gi[:, None]) % SUBLANES
    g2 = gi[:, None]
    g3 = gi[:, None, None]

    def b_mat(x):
        fwd = x[:, 0][:, :, g2, ck - 1 - tau].reshape(depth, ns, gl, S5_ROW, S5_STATE)
        bwd = x[:, 1][:, :, g2, tau].reshape(depth, ns, gl, S5_ROW, S5_STATE)
        return jnp.concatenate([fwd, bwd], axis=-1)

    def c_mat(c):
        fwd = jnp.transpose(c[:, 0][:, :, g2, tau], (0, 1, 2, 5, 3, 4)).reshape(depth, ns, gl, S5_STATE, S5_ROW)
        bwd = jnp.transpose(c[:, 1][:, :, g2, ck - 1 - tau], (0, 1, 2, 5, 3, 4)).reshape(depth, ns, gl, S5_STATE, S5_ROW)
        return jnp.concatenate([fwd, bwd], axis=3).astype(BF16)

    def toeplitz(kd, lag):
        t = kd[:, :, g3, jnp.clip(lag, 0, ck - 1)]
        t = jnp.where((lag >= 0)[None, None, :, :, :, None, None], t, 0.0)
        return jnp.transpose(t, (0, 1, 2, 3, 5, 4, 6)).reshape(depth, ns, gl, S5_ROW, S5_ROW)

    lag_f = tau[:, None, :] - tau[:, :, None]
    tm = toeplitz(kk[:, 0], lag_f) + toeplitz(kk[:, 1], -lag_f)
    w1 = jnp.concatenate([tm, b_mat(xre), b_mat(xim)], axis=-1).astype(BF16)
    a16 = lambda t: jnp.concatenate([t.reshape(depth, 2, ns, gl, S5_STATE)[:, 0],
                                     t.reshape(depth, 2, ns, gl, S5_STATE)[:, 1]], axis=-1)
    return dict(w1=w1, cre=c_mat(cqre), cim=c_mat(cqim), are=a16(a16re), aim=a16(a16im))


def _s5_kernel(z_ref, w1_ref, cre_ref, cim_ref, are_ref, aim_ref, d_ref, y_ref,
               yin, lre, lim, sre_f, sre_b, sim_f, sim_b, *, n_chunks, n_ctx_chunks, pitch):
    ng = GROUPS_PER_SLAB
    blk = lax.broadcasted_iota(jnp.int32, (n_chunks, LANES), 1) // S5_GROUP

    def step_rows(j):
        return z_ref[pl.ds(j, n_chunks, stride=S5_CHUNK), :]

    rot = [[step_rows(SUBLANES * h + s) if s == 0
            else pltpu.roll(step_rows(SUBLANES * h + s), S5_GROUP * s, axis=1)
            for s in range(SUBLANES)] for h in range(2)]
    for g in range(ng):
        halves = []
        for h in range(2):
            u = rot[h][(0 - g) % ng]
            for q in range(1, ng):
                u = jnp.where(blk == q, rot[h][(q - g) % ng], u)
            halves.append(u)
        r = _dot(jnp.concatenate(halves, axis=1).astype(BF16), w1_ref[g])
        yin[g] = r[:, :S5_ROW]
        lre[pl.ds(g * pitch, n_chunks), :] = r[:, S5_ROW:S5_ROW + 2 * S5_STATE]
        lim[pl.ds(g * pitch, n_chunks), :] = r[:, S5_ROW + 2 * S5_STATE:]

    ar = are_ref[...]
    ai = aim_ref[...]
    fwd_lane = lax.broadcasted_iota(jnp.int32, (ng, 2 * S5_STATE), 1) < S5_STATE

    def body(kstep, carry):
        st_re, st_im = carry
        rb = jnp.where(kstep < n_ctx_chunks, n_ctx_chunks - 1 - kstep, n_chunks - 1 + n_ctx_chunks - kstep)
        rows_f = pl.ds(kstep, ng, stride=pitch)
        rows_b = pl.ds(rb, ng, stride=pitch)
        sre_f[rows_f, :] = st_re
        sre_b[rows_b, :] = st_re
        sim_f[rows_f, :] = st_im
        sim_b[rows_b, :] = st_im
        loc_re = jnp.where(fwd_lane, lre[rows_f, :], lre[rows_b, :])
        loc_im = jnp.where(fwd_lane, lim[rows_f, :], lim[rows_b, :])
        return ar * st_re - ai * st_im + loc_re, ar * st_im + ai * st_re + loc_im

    zero = jnp.zeros((ng, 2 * S5_STATE), F32)
    lax.fori_loop(0, n_chunks, body, (zero, zero))

    fwd_big = lax.broadcasted_iota(jnp.int32, (n_chunks, 2 * S5_STATE), 1) < S5_STATE
    for g in range(ng):
        rows = pl.ds(g * pitch, n_chunks)
        s_re = jnp.where(fwd_big, sre_f[rows, :], sre_b[rows, :]).astype(BF16)
        s_im = jnp.where(fwd_big, sim_f[rows, :], sim_b[rows, :]).astype(BF16)
        yin[g] = yin[g] + _dot(s_re, cre_ref[g]) + _dot(s_im, cim_ref[g])

    d = d_ref[...]
    for h in range(2):
        for s in range(SUBLANES):
            v = yin[(0 - s) % ng, :, h * LANES:(h + 1) * LANES]
            for q in range(1, ng):
                v = jnp.where(blk == q, yin[(q - s) % ng, :, h * LANES:(h + 1) * LANES], v)
            if s:
                v = pltpu.roll(v, LANES - S5_GROUP * s, axis=1)
            j = SUBLANES * h + s
            y_ref[pl.ds(j, n_chunks, stride=S5_CHUNK), :] = v + d * step_rows(j)


def _s5_call(s5u, ops, s5_d, layer, nb, n_chunks, n_ctx_chunks):
    t = s5u.shape[0]
    s = t // nb
    pitch = -(-n_chunks // SUBLANES) * SUBLANES
    if (pitch // SUBLANES) % 2 == 0:
        pitch += SUBLANES
    gl = GROUPS_PER_SLAB
    mat = lambda r, c: pl.BlockSpec((None, None, gl, r, c), lambda a, b: (layer, a, 0, 0, 0))
    vec = pl.BlockSpec((None, None, gl, 2 * S5_STATE), lambda a, b: (layer, a, 0, 0))
    state = pltpu.VMEM((gl * pitch, 2 * S5_STATE), F32)
    return pl.pallas_call(
        functools.partial(_s5_kernel, n_chunks=n_chunks, n_ctx_chunks=n_ctx_chunks, pitch=pitch),
        grid=(S5_SLABS, nb),
        in_specs=[pl.BlockSpec((s, LANES), lambda a, b: (b, a)),
                  mat(S5_ROW, S5_ROW + 4 * S5_STATE), mat(2 * S5_STATE, S5_ROW), mat(2 * S5_STATE, S5_ROW),
                  vec, vec, pl.BlockSpec((1, LANES), lambda a, b: (0, a))],
        out_specs=pl.BlockSpec((s, LANES), lambda a, b: (b, a)),
        out_shape=jax.ShapeDtypeStruct((t, S5_WIDTH), F32),
        scratch_shapes=[pltpu.VMEM((gl, n_chunks, S5_ROW), F32)] + [state] * 6,
        compiler_params=_cparams(2),
        name="s5_scan",
    )(s5u, ops["w1"], ops["cre"], ops["cim"], ops["are"], ops["aim"], s5_d)


def _inproj_kernel(ctx_ref, lat_ref, mod_ref, g_ref, w_ref, qn_ref, kvn_ref, wq_ref, wkv_ref, e_ref, rope_ref,
                   qc_out, ql_out, k_out, v_out, cb_out, uc_out, s5_out, gate_out):
    mod = mod_ref[0]
    x = jnp.where(pl.program_id(1) == 0, ctx_ref[...], lat_ref[...])
    xn = _rms(x, g_ref[...]) * (1.0 + mod[:, D_MODEL:]) + mod[:, :D_MODEL]
    xn = xn.astype(BF16)

    def proj(a, n):
        return _dot(xn, w_ref[:, a:a + n])

    zqkv = proj(P_QKV, Q_LORA + KV_LORA)
    rope = rope_ref[...]
    cos_q = rope[:, 0:LANES]
    sin_q = rope[:, LANES:2 * LANES]
    tab_k = rope[:, 2 * LANES:3 * LANES]
    qn = _rms(zqkv[:, :Q_LORA], qn_ref[...]).astype(BF16)
    q12 = _dot(qn, wq_ref[...])
    nq = N_HEADS * HEAD_PAD
    heads = []
    for h in range(N_HEADS):
        a = h * HEAD_PAD
        qh = q12[:, a:a + HEAD_PAD] * cos_q + q12[:, nq + a:nq + a + HEAD_PAD] * sin_q
        heads.append((qh * Q_SCALE).astype(BF16))
    q = jnp.concatenate(heads, axis=1)

    @pl.when(pl.program_id(1) == 0)
    def _():
        qc_out[...] = q

    @pl.when(pl.program_id(1) != 0)
    def _():
        ql_out[...] = q

    kvn = _rms(zqkv[:, Q_LORA:], kvn_ref[...]).astype(BF16)
    zpe = proj(P_PE, LANES)
    kpe = _dot((zpe * tab_k).astype(BF16), e_ref[...])
    kv = _dot(kvn, wkv_ref[...])
    k_out[...] = (kv[:, :nq] + kpe).astype(BF16)
    one_lane = lax.broadcasted_iota(jnp.int32, (1, nq), 1) % HEAD_PAD == V_DIM
    v_out[...] = (kv[:, nq:] + jnp.where(one_lane, 1.0, 0.0)).astype(BF16)
    cb_out[...] = proj(P_CB, CONV_WIDTH).astype(BF16)
    uc_out[...] = (proj(P_CC, CONV_WIDTH) * proj(P_CX, CONV_WIDTH)).astype(BF16)
    s5_out[...] = proj(P_S5, S5_WIDTH)
    gate_out[...] = _sigmoid(proj(P_G, N_BRANCH * D_MODEL)).astype(BF16)


def _token_specs(src, first_tile, width=D_MODEL):
    return [pl.BlockSpec((TM, width), lambda b, i: (b * src.ctx_stride, 0)),
            pl.BlockSpec((TM, width),
                         lambda b, i: (b * src.lat_stride + src.lat_off + jnp.maximum(i + first_tile - 1, 0), 0))]


def _inproj_call(src, mods, lw, rope, e_mat, nb, tps):
    t = nb * tps * TM
    rowb = lambda n: pl.BlockSpec((TM, n), lambda b, i: (b * tps + i, 0))
    nq = N_HEADS * HEAD_PAD
    outs = [(nq, BF16), (nq, BF16), (CONV_WIDTH, BF16), (CONV_WIDTH, BF16), (S5_WIDTH, F32),
            (N_BRANCH * D_MODEL, BF16)]
    q_specs = [pl.BlockSpec((TM, nq), lambda b, i: (b, 0)),
               pl.BlockSpec((TM, nq), lambda b, i: (b * (tps - 1) + jnp.maximum(i - 1, 0), 0))]
    q_shapes = [jax.ShapeDtypeStruct((nb * TM, nq), BF16), jax.ShapeDtypeStruct((nb * (tps - 1) * TM, nq), BF16)]
    return pl.pallas_call(
        _inproj_kernel,
        grid=(nb, tps),
        in_specs=_token_specs(src, 0) + [
                  pl.BlockSpec((1, 1, 2 * D_MODEL), lambda b, i: (jnp.where(i == 0, nb, b), 0, 0)),
                  _const_spec((1, D_MODEL)),
                  _const_spec((D_MODEL, P_COLS)),
                  _const_spec((1, Q_LORA)), _const_spec((1, KV_LORA)),
                  _const_spec((Q_LORA, 2 * nq)), _const_spec((KV_LORA, 2 * nq)),
                  _const_spec((LANES, nq)),
                  pl.BlockSpec((TM, 3 * LANES), lambda b, i: (i, 0))],
        out_specs=q_specs + [rowb(n) for n, _ in outs],
        out_shape=q_shapes + [jax.ShapeDtypeStruct((t, n), dt) for n, dt in outs],
        compiler_params=_cparams(2),
        name="in_projection",
    )(src.ctx, src.lat, mods, lw["norm_mix"], lw["w_in"], lw["q_norm"], lw["kv_norm"], lw["wq"], lw["wkv"],
      e_mat, rope)


def _attn_head(q, k, v):
    s = lax.dot_general(q, k, (((1,), (1,)), ((), ())), preferred_element_type=F32)
    p = jnp.exp2(s - jnp.max(s, axis=-1, keepdims=True)).astype(BF16)
    oe = _dot(p, v)
    return oe[:, :V_DIM] / oe[:, V_DIM:V_DIM + 1]


def _attn_kernel(q_ref, k_ref, v_ref, o_ref):
    outs = []
    for h in range(N_HEADS):
        sl = slice(h * HEAD_PAD, (h + 1) * HEAD_PAD)
        outs.append(_attn_head(q_ref[:, sl], k_ref[:, sl], v_ref[:, sl]))
    o_ref[...] = jnp.concatenate(outs, axis=-1).astype(BF16)


def _attn_call(q, k, v, nb, rows_q, rows_kv, kv_stride, name):
    nq = N_HEADS * HEAD_PAD
    nv = N_HEADS * V_DIM
    n_q = q.shape[0] // (nb * rows_q)
    kv_spec = pl.BlockSpec((rows_kv, nq), lambda b, i: (b * (kv_stride // rows_kv), 0), pipeline_mode=pl.Buffered(1))
    return pl.pallas_call(
        _attn_kernel,
        grid=(nb, n_q),
        in_specs=[pl.BlockSpec((rows_q, nq), lambda b, i: (b * n_q + i, 0)), kv_spec, kv_spec],
        out_specs=pl.BlockSpec((rows_q, nv), lambda b, i: (b * n_q + i, 0)),
        out_shape=jax.ShapeDtypeStruct((q.shape[0], nv), BF16),
        compiler_params=_cparams(2),
        name=name,
    )(q, k, v)


def _gelu_tanh(x):
    return 0.5 * x * (1.0 + jnp.tanh(math.sqrt(2.0 / math.pi) * (x + 0.044715 * (x * x * x))))


def _merge_kernel(ctx_ref, lat_ref, mod_ref, oc_ref, ol_ref, cb_ref, uc_ref, ucp_ref, ucn_ref, ys_ref, gate_ref,
                  wo_ref, cw_ref, cwo_ref, wglu_ref, wout_ref, out_ref, *, first_tile, tps):
    i = pl.program_id(1) + first_tile
    att = _dot(jnp.where(i == 0, oc_ref[...], ol_ref[...]), wo_ref[...])
    uc = uc_ref[...].astype(F32)
    prev_row = jnp.where(i >= 2, ucp_ref[HALO - 1:HALO, :].astype(F32), 0.0)
    next_row = jnp.where(jnp.logical_and(i >= 1, i < tps - 1), ucn_ref[0:1, :].astype(F32), 0.0)
    row = lax.broadcasted_iota(jnp.int32, (TM, 1), 0)
    up = jnp.where(row == 0, prev_row, pltpu.roll(uc, 1, axis=0))
    dn = jnp.where(row == TM - 1, next_row, pltpu.roll(uc, TM - 1, axis=0))
    y = up * cw_ref[0:1, :] + uc * cw_ref[1:2, :] + dn * cw_ref[2:3, :]
    conv = _dot((cb_ref[...].astype(F32) * y).astype(BF16), cwo_ref[...])
    z = _dot(_gelu_tanh(ys_ref[...]).astype(BF16), wglu_ref[...])
    s5o = z[:, :D_MODEL] * _sigmoid(z[:, D_MODEL:])
    g = gate_ref[...].astype(F32)
    merged = g[:, :D_MODEL] * att + g[:, D_MODEL:2 * D_MODEL] * conv + g[:, 2 * D_MODEL:] * s5o
    x = jnp.where(i == 0, ctx_ref[...], lat_ref[...])
    out_ref[...] = x + mod_ref[0] * _dot(merged.astype(BF16), wout_ref[...])


def _merge_call(src, mods, o_src, cb, uc, ys, gate, lw, nb, tps, first_tile):
    t = cb.shape[0]
    blk = lambda b, i: b * tps + i + first_tile
    rowb = lambda n: pl.BlockSpec((TM, n), lambda b, i: (blk(b, i), 0))
    per_halo = TM // HALO
    last_halo = t // HALO - 1
    return pl.pallas_call(
        functools.partial(_merge_kernel, first_tile=first_tile, tps=tps),
        grid=(nb, tps - first_tile),
        in_specs=_token_specs(src, first_tile) + [
                  pl.BlockSpec((1, 1, D_MODEL), lambda b, i: (jnp.where(i + first_tile == 0, nb, b), 0, 2))]
                 + _token_specs(o_src, first_tile, N_HEADS * V_DIM) + [
                  rowb(CONV_WIDTH), rowb(CONV_WIDTH),
                  pl.BlockSpec((HALO, CONV_WIDTH), lambda b, i: (jnp.maximum(blk(b, i) * per_halo - 1, 0), 0)),
                  pl.BlockSpec((HALO, CONV_WIDTH),
                               lambda b, i: (jnp.minimum((blk(b, i) + 1) * per_halo, last_halo), 0)),
                  rowb(S5_WIDTH), rowb(N_BRANCH * D_MODEL),
                  _const_spec((N_HEADS * V_DIM, D_MODEL)), _const_spec((CONV_K, CONV_WIDTH)),
                  _const_spec((CONV_WIDTH, D_MODEL)),
                  _const_spec((S5_WIDTH, 2 * D_MODEL)), _const_spec((D_MODEL, D_MODEL))],
        out_specs=rowb(D_MODEL),
        out_shape=jax.ShapeDtypeStruct((t, D_MODEL), F32),
        compiler_params=_cparams(2),
        name="branch_merge",
    )(src.ctx, src.lat, mods, o_src.ctx, o_src.lat, cb, uc, uc, uc, ys, gate,
      lw["w_o"], lw["conv_w"], lw["conv_w_out"], lw["w_glu"], lw["w_out"])


def _mlp_kernel(x_ref, mod_ref, g_ref, w1_ref, w2_ref, gf_ref, out_ref, *, final):
    x = x_ref[...]
    mod = mod_ref[0]
    h = (_rms(x, g_ref[...]) * (1.0 + mod[:, D_MODEL:2 * D_MODEL]) + mod[:, :D_MODEL]).astype(BF16)
    acc = jnp.zeros((TM, D_MODEL), F32)
    for c in range(D_FF // D_MODEL):
        a = jnp.maximum(_dot(h, w1_ref[:, c * D_MODEL:(c + 1) * D_MODEL]), 0.0)
        acc = acc + _dot((a * a).astype(BF16), w2_ref[c * D_MODEL:(c + 1) * D_MODEL, :])
    y = x + mod[:, 2 * D_MODEL:] * acc
    if final:
        y = _rms(y, gf_ref[...])
    out_ref[...] = y


def _mlp_call(x1, mods, lw, norm_final, nb, tps, first_tile, final):
    n_tiles = tps - first_tile
    return pl.pallas_call(
        functools.partial(_mlp_kernel, final=final),
        grid=(nb, n_tiles),
        in_specs=[pl.BlockSpec((TM, D_MODEL), lambda b, i: (b * tps + i + first_tile, 0)),
                  pl.BlockSpec((1, 1, 3 * D_MODEL), lambda b, i: (jnp.where(i + first_tile == 0, nb, b), 0, 1)),
                  _const_spec((1, D_MODEL)),
                  _const_spec((D_MODEL, D_FF)), _const_spec((D_FF, D_MODEL)),
                  _const_spec((1, D_MODEL))],
        out_specs=pl.BlockSpec((TM, D_MODEL), lambda b, i: (b * n_tiles + i, 0)),
        out_shape=jax.ShapeDtypeStruct((nb * n_tiles * TM, D_MODEL), F32),
        compiler_params=_cparams(2),
        name="mlp",
    )(x1, mods, lw["norm_mlp"], lw["w1"], lw["w2"], norm_final)


def _rot_half(w):
    wr = w.reshape(w.shape[:-1] + (2, 2, QK_ROPE // 4))
    return jnp.concatenate([-wr[..., 1:, :], wr[..., :1, :]], axis=-2).reshape(w.shape)


def _layer_weights(i, w_in, norm_mix, q_norm, w_uq, kv_norm, w_ukv, w_o, conv_w, conv_w_out, s5_d, w_glu,
                   w_out, norm_mlp, w1, w2):
    wi = w_in[i]
    pe = wi[:, OFF_PE:OFF_CB]
    w_in_p = jnp.concatenate(
        [wi[:, OFF_Q:OFF_PE], wi[:, OFF_CB:OFF_G], wi[:, OFF_G:], pe, _rot_half(pe),
         jnp.zeros((D_MODEL, LANES - 2 * QK_ROPE), F32)], axis=1).astype(BF16)
    uq = w_uq[i].reshape(Q_LORA, N_HEADS, QK_NOPE + QK_ROPE)
    zpad = jnp.zeros((Q_LORA, N_HEADS, HEAD_PAD - QK_NOPE - QK_ROPE), F32)
    wq1 = jnp.concatenate([uq, zpad], axis=-1).reshape(Q_LORA, N_HEADS * HEAD_PAD)
    wq2 = jnp.concatenate([jnp.zeros((Q_LORA, N_HEADS, QK_NOPE), F32), _rot_half(uq[..., QK_NOPE:]), zpad],
                          axis=-1).reshape(Q_LORA, N_HEADS * HEAD_PAD)
    ukv = w_ukv[i].reshape(KV_LORA, N_HEADS, QK_NOPE + V_DIM)
    wk = jnp.concatenate([ukv[..., :QK_NOPE], jnp.zeros((KV_LORA, N_HEADS, HEAD_PAD - QK_NOPE), F32)],
                         axis=-1).reshape(KV_LORA, N_HEADS * HEAD_PAD)
    wv = jnp.concatenate([ukv[..., QK_NOPE:], jnp.zeros((KV_LORA, N_HEADS, HEAD_PAD - V_DIM), F32)],
                         axis=-1).reshape(KV_LORA, N_HEADS * HEAD_PAD)
    return dict(
        w_in=w_in_p, norm_mix=norm_mix[i].reshape(1, D_MODEL),
        q_norm=q_norm[i].reshape(1, Q_LORA), kv_norm=kv_norm[i].reshape(1, KV_LORA),
        wq=jnp.concatenate([wq1, wq2], axis=1).astype(BF16),
        wkv=jnp.concatenate([wk, wv], axis=1).astype(BF16),
        w_o=w_o[i].astype(BF16), conv_w=conv_w[i], conv_w_out=conv_w_out[i].astype(BF16),
        s5_d=s5_d[i].reshape(1, S5_WIDTH), w_glu=w_glu[i].astype(BF16), w_out=w_out[i].astype(BF16),
        norm_mlp=norm_mlp[i].reshape(1, D_MODEL), w1=w1[i].astype(BF16), w2=w2[i].astype(BF16))


def _rope_tables(n_ctx, n_tokens):
    rows = n_tokens // GRID_W
    pos = jnp.stack([jnp.repeat(jnp.arange(rows), GRID_W), jnp.tile(jnp.arange(GRID_W), rows)], -1).astype(F32)
    n_freq = QK_ROPE // 4
    inv = ROPE_THETA ** (-jnp.arange(n_freq, dtype=F32) / n_freq)
    ang = pos[:, :, None, None] * inv[None, None, None, :]
    ang = jnp.broadcast_to(ang, (n_tokens, 2, 2, n_freq)).reshape(n_tokens, QK_ROPE)
    cos = jnp.concatenate([jnp.ones((n_ctx, QK_ROPE), F32), jnp.cos(ang)], axis=0)
    sin = jnp.concatenate([jnp.zeros((n_ctx, QK_ROPE), F32), jnp.sin(ang)], axis=0)
    s = n_ctx + n_tokens
    one = jnp.ones((s, QK_NOPE), F32)
    z = lambda n: jnp.zeros((s, n), F32)
    pad = HEAD_PAD - QK_NOPE - QK_ROPE
    return jnp.concatenate([one, cos, z(pad), z(QK_NOPE), sin, z(pad), cos, sin, z(LANES - 2 * QK_ROPE)], axis=1)


def _pe_spread():
    j = jnp.arange(QK_ROPE)
    e = jnp.zeros((LANES, N_HEADS * HEAD_PAD), F32)
    for h in range(N_HEADS):
        e = e.at[j, h * HEAD_PAD + QK_NOPE + j].set(1.0)
        e = e.at[QK_ROPE + j, h * HEAD_PAD + QK_NOPE + j].set(1.0)
    return e.astype(BF16)


def kernel(x, c, ctx, c_ctx, ada_w, ada_b, norm_mix, w_in, mla_q_norm, mla_w_uq, mla_kv_norm, mla_w_ukv, mla_w_o, conv_w, conv_w_out, s5_a_re, s5_a_im, s5_log_dt, s5_b_re, s5_b_im, s5_c_re, s5_c_im, s5_d, s5_w_glu, w_out, norm_mlp, mlp_w1, mlp_w2, norm_final):
    nb, n_lat, _ = x.shape
    n_ctx = ctx.shape[1]
    depth = ada_w.shape[0]
    assert n_ctx == TM and n_lat % TQ == 0 and n_lat % GRID_W == 0 and nb == SUBLANES
    s = n_ctx + n_lat
    tps = s // TM
    t = nb * s
    n_chunks = s // S5_CHUNK
    n_ctx_chunks = n_ctx // S5_CHUNK

    c16 = jnp.zeros((16, D_MODEL), F32).at[:nb].set(c).at[nb].set(c_ctx)
    mods = _ada_call(c16, ada_w, ada_b)[:, :nb + 1].reshape(depth, nb + 1, 1, N_MOD * D_MODEL)
    ops = _s5_operators(s5_a_re, s5_a_im, s5_log_dt, s5_b_re, s5_b_im, s5_c_re, s5_c_im)
    rope = _rope_tables(n_ctx, n_lat)
    e_mat = _pe_spread()
    src = _TokenSource(ctx.reshape(nb * n_ctx, D_MODEL), x.reshape(nb * n_lat, D_MODEL), 1, tps - 1, 0)
    gf = norm_final.reshape(1, D_MODEL)

    for i in range(depth):
        last = i == depth - 1
        ft = 1 if last else 0
        lw = _layer_weights(i, w_in, norm_mix, mla_q_norm, mla_w_uq, mla_kv_norm, mla_w_ukv, mla_w_o, conv_w,
                            conv_w_out, s5_d, s5_w_glu, w_out, norm_mlp, mlp_w1, mlp_w2)
        q_ctx, q_lat, k, v, cb, uc, s5u, gate = _inproj_call(src, mods[i], lw, rope, e_mat, nb, tps)
        o_lat = _attn_call(q_lat, k, v, nb, TQ, s, s, "attention")
        o_ctx = o_lat if last else _attn_call(q_ctx, k, v, nb, TM, n_ctx, s, "attention_ctx")
        o_src = _TokenSource(o_ctx, o_lat, 1, tps - 1, 0)
        ys = _s5_call(s5u, ops, lw["s5_d"], i, nb, n_chunks, n_ctx_chunks)
        x1 = _merge_call(src, mods[i], o_src, cb, uc, ys, gate, lw, nb, tps, ft)
        xs = _mlp_call(x1, mods[i], lw, gf, nb, tps, ft, last)
        src = _TokenSource(xs, xs, tps, tps, 1)
    return xs.reshape(nb, n_lat, D_MODEL)
```

```python
import functools
import math
from typing import NamedTuple

import jax
import jax.numpy as jnp
from jax import lax
from jax.experimental import pallas as pl
from jax.experimental.pallas import tpu as pltpu

F32 = jnp.float32
BF16 = jnp.bfloat16

D_MODEL = 1024
GRID_W = 64
N_HEADS = 8
QK_NOPE = 64
QK_ROPE = 32
V_DIM = 64
Q_LORA = 256
KV_LORA = 256
ROPE_THETA = 10000.0
ATTN_SCALE = 1.0 / math.sqrt(QK_NOPE + QK_ROPE)
Q_SCALE = ATTN_SCALE * math.log2(math.e)
CONV_WIDTH = 512
CONV_K = 3
S5_WIDTH = 512
S5_GROUP = 16
S5_GROUPS = S5_WIDTH // S5_GROUP
S5_STATE = 64
N_BRANCH = 3
D_FF = 4 * D_MODEL
EPS = 1e-6
N_MOD = 6

OFF_Q = 0
OFF_KV = OFF_Q + Q_LORA
OFF_PE = OFF_KV + KV_LORA
OFF_CB = OFF_PE + QK_ROPE
OFF_CC = OFF_CB + CONV_WIDTH
OFF_CX = OFF_CC + CONV_WIDTH
OFF_S5 = OFF_CX + CONV_WIDTH
OFF_G = OFF_S5 + S5_WIDTH
IN_COLS = OFF_G + N_BRANCH * D_MODEL

LANES = 128
SUBLANES = 8
HEAD_PAD = LANES

P_QKV = 0
P_CB = P_QKV + Q_LORA + KV_LORA
P_CC = P_CB + CONV_WIDTH
P_CX = P_CC + CONV_WIDTH
P_S5 = P_CX + CONV_WIDTH
P_G = P_S5 + S5_WIDTH
P_PE = P_G + N_BRANCH * D_MODEL
P_COLS = P_PE + LANES

TM = 256
TQ = 512
HALO = 16
S5_CHUNK = 16
S5_ROW = S5_CHUNK * S5_GROUP
GROUPS_PER_SLAB = LANES // S5_GROUP
S5_SLABS = S5_WIDTH // LANES
VMEM_LIMIT = 56 * 1024 * 1024


class _TokenSource(NamedTuple):
    ctx: jax.Array
    lat: jax.Array
    ctx_stride: int
    lat_stride: int
    lat_off: int


def _cparams(n_axes):
    return pltpu.CompilerParams(dimension_semantics=("arbitrary",) * n_axes,
                                vmem_limit_bytes=VMEM_LIMIT)


def _const_spec(shape):
    nd = len(shape)
    return pl.BlockSpec(shape, lambda *_: (0,) * nd)


def _rms(x, g):
    return x * lax.rsqrt(jnp.mean(x * x, axis=-1, keepdims=True) + EPS) * g


def _sigmoid(x):
    return 1.0 / (1.0 + jnp.exp(-x))


def _dot(a, b):
    return jnp.dot(a, b, preferred_element_type=F32)


def _ada_kernel(c_ref, w_ref, b_ref, o_ref):
    c = c_ref[...]
    s = (c * _sigmoid(c)).astype(BF16)
    o_ref[0] = _dot(s, w_ref[0].astype(BF16)) + b_ref[0]


def _ada_call(c16, ada_w, ada_b):
    depth = ada_w.shape[0]
    n_col = N_MOD * D_MODEL
    bn = n_col // 4
    return pl.pallas_call(
        _ada_kernel,
        grid=(depth, n_col // bn),
        in_specs=[pl.BlockSpec((16, D_MODEL), lambda l, j: (0, 0)),
                  pl.BlockSpec((1, D_MODEL, bn), lambda l, j: (l, 0, j)),
                  pl.BlockSpec((1, 1, bn), lambda l, j: (l, 0, j))],
        out_specs=pl.BlockSpec((1, 16, bn), lambda l, j: (l, 0, j)),
        out_shape=jax.ShapeDtypeStruct((depth, 16, n_col), F32),
        compiler_params=_cparams(2),
        name="ada_mod",
    )(c16, ada_w, ada_b.reshape(depth, 1, n_col))


def _shift_blocks(x, m):
    lo, hi = x[:, :LANES], x[:, LANES:]
    lane = lax.broadcasted_iota(jnp.int32, lo.shape, 1)
    zero = jnp.zeros_like(lo)
    if m == 0:
        return x
    k = abs(m) % GROUPS_PER_SLAB
    if m > 0:
        cut = S5_GROUP * k
        if m < GROUPS_PER_SLAB:
            r_lo, r_hi = pltpu.roll(lo, cut, axis=1), pltpu.roll(hi, cut, axis=1)
            out = [jnp.where(lane >= cut, r_lo, zero), jnp.where(lane >= cut, r_hi, r_lo)]
        elif k == 0:
            out = [zero, lo]
        else:
            out = [zero, jnp.where(lane >= cut, pltpu.roll(lo, cut, axis=1), zero)]
    else:
        cut = LANES - S5_GROUP * k
        if -m < GROUPS_PER_SLAB:
            r_lo, r_hi = pltpu.roll(lo, cut, axis=1), pltpu.roll(hi, cut, axis=1)
            out = [jnp.where(lane < cut, r_lo, r_hi), jnp.where(lane < cut, r_hi, zero)]
        elif k == 0:
            out = [hi, zero]
        else:
            out = [jnp.where(lane < cut, pltpu.roll(hi, cut, axis=1), zero), zero]
    return jnp.concatenate(out, axis=1)


def _s5prep_kernel(are_ref, aim_ref, ldt_ref, bre_ref, bim_ref, cre_ref, cim_ref,
                   w1_ref, cmre_ref, cmim_ref, a16re_ref, a16im_ref):
    ck = S5_CHUNK
    rows = lax.broadcasted_iota(jnp.int32, (S5_ROW, 1), 0) // S5_GROUP
    fwd = lax.broadcasted_iota(jnp.int32, (1, 2 * S5_STATE), 1) < S5_STATE
    dn = (((1,), (1,)), ((), ()))
    hi = lax.Precision.HIGHEST
    for g in range(GROUPS_PER_SLAB):
        are = are_ref[0, g:g + 1, :]
        aim = aim_ref[0, g:g + 1, :]
        dt = jnp.exp(ldt_ref[0, g:g + 1, :])

        def power(p):
            m = jnp.exp(p * (dt * are))
            ang = p * (dt * aim)
            return m * jnp.cos(ang), m * jnp.sin(ang)

        abre, abim = power(1.0)
        den = are * are + aim * aim
        nr = abre - 1.0
        fre = (nr * are + abim * aim) / den
        fim = (abim * are - nr * aim) / den
        b_re, b_im = bre_ref[0, g], bim_ref[0, g]
        bb_re = fre * b_re - fim * b_im
        bb_im = fre * b_im + fim * b_re
        tile = lambda t: jnp.concatenate([t] * ck, axis=0)
        step = SUBLANES * (rows // SUBLANES) + ((rows % SUBLANES - g) & (SUBLANES - 1))
        stepf = step.astype(F32)
        p_re, p_im = power(jnp.where(fwd, float(ck - 1) - stepf, stepf))
        t_re, t_im = tile(bb_re), tile(bb_im)
        w1_ref[0, g, :, S5_ROW:S5_ROW + 2 * S5_STATE] = (p_re * t_re - p_im * t_im).astype(BF16)
        w1_ref[0, g, :, S5_ROW + 2 * S5_STATE:] = (p_re * t_im + p_im * t_re).astype(BF16)
        c_re, c_im = tile(cre_ref[0, g]), tile(cim_ref[0, g])
        p_re, p_im = power(jnp.where(fwd, stepf + 1.0, float(ck) - stepf))
        cmre_ref[0, g] = (c_re * p_re - c_im * p_im).T.astype(BF16)
        cmim_ref[0, g] = (-(c_re * p_im + c_im * p_re)).T.astype(BF16)
        lag = rows.astype(F32)
        p_re, p_im = power(jnp.where(fwd, lag, float(ck - 1) - lag))
        x_re = p_re * c_re - p_im * c_im
        x_im = p_re * c_im + p_im * c_re

        def kernels(mask):
            return (lax.dot_general(jnp.where(mask, bb_re, 0.0), x_re, dn, precision=hi, preferred_element_type=F32)
                    - lax.dot_general(jnp.where(mask, bb_im, 0.0), x_im, dn, precision=hi,
                                      preferred_element_type=F32))

        kf = kernels(fwd)
        kb = kernels(jnp.logical_not(fwd))
        for pos in range(ck):
            s = SUBLANES * (pos // SUBLANES) + (pos % SUBLANES - g) % SUBLANES
            blockrow = _shift_blocks(kf, s) + _shift_blocks(kb, s - (ck - 1))
            halves = [blockrow[:, :LANES], blockrow[:, LANES:]]
            if g:
                halves = [pltpu.roll(t, S5_GROUP * g, axis=1) for t in halves]
            w1_ref[0, g, pos * S5_GROUP:(pos + 1) * S5_GROUP, :S5_ROW] = jnp.concatenate(halves, axis=1).astype(BF16)
        e16 = jnp.exp(float(ck) * (dt * are))
        a16re_ref[0, g:g + 1, :] = e16 * jnp.cos(float(ck) * (dt * aim))
        a16im_ref[0, g:g + 1, :] = e16 * jnp.sin(float(ck) * (dt * aim))


def _s5_operators(a_re, a_im, log_dt, b_re, b_im, c_re, c_im):
    depth = a_re.shape[0]
    n = depth * S5_SLABS
    gl = GROUPS_PER_SLAB
    vec = lambda t: jnp.moveaxis(t, 1, 2).reshape(n, gl, 2 * S5_STATE)
    ldt = jnp.broadcast_to(jnp.moveaxis(log_dt, 1, 2)[..., None], (depth, S5_GROUPS, 2, S5_STATE))
    bmat = lambda t: jnp.transpose(t, (0, 2, 4, 1, 3)).reshape(n, gl, S5_GROUP, 2 * S5_STATE)
    cmat = lambda t: jnp.transpose(t, (0, 2, 3, 1, 4)).reshape(n, gl, S5_GROUP, 2 * S5_STATE)
    vspec = pl.BlockSpec((1, gl, 2 * S5_STATE), lambda m: (m, 0, 0))
    mspec = pl.BlockSpec((1, gl, S5_GROUP, 2 * S5_STATE), lambda m: (m, 0, 0, 0))
    w1, cre, cim, a16re, a16im = pl.pallas_call(
        _s5prep_kernel,
        grid=(n,),
        in_specs=[vspec, vspec, vspec, mspec, mspec, mspec, mspec],
        out_specs=[pl.BlockSpec((1, gl, S5_ROW, S5_ROW + 4 * S5_STATE), lambda m: (m, 0, 0, 0)),
                   pl.BlockSpec((1, gl, 2 * S5_STATE, S5_ROW), lambda m: (m, 0, 0, 0)),
                   pl.BlockSpec((1, gl, 2 * S5_STATE, S5_ROW), lambda m: (m, 0, 0, 0)), vspec, vspec],
        out_shape=[jax.ShapeDtypeStruct((n, gl, S5_ROW, S5_ROW + 4 * S5_STATE), BF16),
                   jax.ShapeDtypeStruct((n, gl, 2 * S5_STATE, S5_ROW), BF16),
                   jax.ShapeDtypeStruct((n, gl, 2 * S5_STATE, S5_ROW), BF16),
                   jax.ShapeDtypeStruct((n, gl, 2 * S5_STATE), F32), jax.ShapeDtypeStruct((n, gl, 2 * S5_STATE), F32)],
        compiler_params=_cparams(1),
        name="s5_operators",
    )(vec(a_re), vec(a_im), ldt.reshape(n, gl, 2 * S5_STATE), bmat(b_re), bmat(b_im), cmat(c_re), cmat(c_im))
    r = lambda t: t.reshape((depth, S5_SLABS) + t.shape[1:])
    return dict(w1=r(w1), cre=r(cre), cim=r(cim), are=r(a16re), aim=r(a16im))


def _s5_kernel(z_ref, w1_ref, cre_ref, cim_ref, are_ref, aim_ref, d_ref, y_ref,
               yin, lre, lim, sre_f, sre_b, sim_f, sim_b, *, n_chunks, n_ctx_chunks, pitch):
    ng = GROUPS_PER_SLAB
    blk = lax.broadcasted_iota(jnp.int32, (n_chunks, LANES), 1) // S5_GROUP

    def step_rows(j):
        return z_ref[pl.ds(j, n_chunks, stride=S5_CHUNK), :]

    rot = [[step_rows(SUBLANES * h + s) if s == 0
            else pltpu.roll(step_rows(SUBLANES * h + s), S5_GROUP * s, axis=1)
            for s in range(SUBLANES)] for h in range(2)]
    for g in range(ng):
        halves = []
        for h in range(2):
            u = rot[h][(0 - g) % ng]
            for q in range(1, ng):
                u = jnp.where(blk == q, rot[h][(q - g) % ng], u)
            halves.append(u)
        r = _dot(jnp.concatenate(halves, axis=1).astype(BF16), w1_ref[g])
        yin[g] = r[:, :S5_ROW]
        lre[pl.ds(g * pitch, n_chunks), :] = r[:, S5_ROW:S5_ROW + 2 * S5_STATE]
        lim[pl.ds(g * pitch, n_chunks), :] = r[:, S5_ROW + 2 * S5_STATE:]

    ar = are_ref[...]
    ai = aim_ref[...]
    fwd_lane = lax.broadcasted_iota(jnp.int32, (ng, 2 * S5_STATE), 1) < S5_STATE

    def body(kstep, carry):
        st_re, st_im = carry
        rb = jnp.where(kstep < n_ctx_chunks, n_ctx_chunks - 1 - kstep, n_chunks - 1 + n_ctx_chunks - kstep)
        rows_f = pl.ds(kstep, ng, stride=pitch)
        rows_b = pl.ds(rb, ng, stride=pitch)
        sre_f[rows_f, :] = st_re
        sre_b[rows_b, :] = st_re
        sim_f[rows_f, :] = st_im
        sim_b[rows_b, :] = st_im
        loc_re = jnp.where(fwd_lane, lre[rows_f, :], lre[rows_b, :])
        loc_im = jnp.where(fwd_lane, lim[rows_f, :], lim[rows_b, :])
        return ar * st_re - ai * st_im + loc_re, ar * st_im + ai * st_re + loc_im

    zero = jnp.zeros((ng, 2 * S5_STATE), F32)
    lax.fori_loop(0, n_chunks, body, (zero, zero))

    fwd_big = lax.broadcasted_iota(jnp.int32, (n_chunks, 2 * S5_STATE), 1) < S5_STATE
    for g in range(ng):
        rows = pl.ds(g * pitch, n_chunks)
        s_re = jnp.where(fwd_big, sre_f[rows, :], sre_b[rows, :]).astype(BF16)
        s_im = jnp.where(fwd_big, sim_f[rows, :], sim_b[rows, :]).astype(BF16)
        yin[g] = yin[g] + _dot(s_re, cre_ref[g]) + _dot(s_im, cim_ref[g])

    d = d_ref[...]
    for h in range(2):
        for s in range(SUBLANES):
            v = yin[(0 - s) % ng, :, h * LANES:(h + 1) * LANES]
            for q in range(1, ng):
                v = jnp.where(blk == q, yin[(q - s) % ng, :, h * LANES:(h + 1) * LANES], v)
            if s:
                v = pltpu.roll(v, LANES - S5_GROUP * s, axis=1)
            j = SUBLANES * h + s
            y_ref[pl.ds(j, n_chunks, stride=S5_CHUNK), :] = v + d * step_rows(j)


def _s5_call(s5u, ops, s5_d, layer, nb, n_chunks, n_ctx_chunks):
    t = s5u.shape[0]
    s = t // nb
    pitch = -(-n_chunks // SUBLANES) * SUBLANES
    if (pitch // SUBLANES) % 2 == 0:
        pitch += SUBLANES
    gl = GROUPS_PER_SLAB
    mat = lambda r, c: pl.BlockSpec((None, None, gl, r, c), lambda a, b: (layer, a, 0, 0, 0))
    vec = pl.BlockSpec((None, None, gl, 2 * S5_STATE), lambda a, b: (layer, a, 0, 0))
    state = pltpu.VMEM((gl * pitch, 2 * S5_STATE), F32)
    return pl.pallas_call(
        functools.partial(_s5_kernel, n_chunks=n_chunks, n_ctx_chunks=n_ctx_chunks, pitch=pitch),
        grid=(S5_SLABS, nb),
        in_specs=[pl.BlockSpec((s, LANES), lambda a, b: (b, a)),
                  mat(S5_ROW, S5_ROW + 4 * S5_STATE), mat(2 * S5_STATE, S5_ROW), mat(2 * S5_STATE, S5_ROW),
                  vec, vec, pl.BlockSpec((1, LANES), lambda a, b: (0, a))],
        out_specs=pl.BlockSpec((s, LANES), lambda a, b: (b, a)),
        out_shape=jax.ShapeDtypeStruct((t, S5_WIDTH), F32),
        scratch_shapes=[pltpu.VMEM((gl, n_chunks, S5_ROW), F32)] + [state] * 6,
        compiler_params=_cparams(2),
        name="s5_scan",
    )(s5u, ops["w1"], ops["cre"], ops["cim"], ops["are"], ops["aim"], s5_d)


def _inproj_kernel(ctx_ref, lat_ref, mod_ref, g_ref, w_ref, qn_ref, kvn_ref, wq_ref, wkv_ref, e_ref, rope_ref,
                   qc_out, ql_out, k_out, v_out, cb_out, uc_out, s5_out, gate_out):
    mod = mod_ref[0]
    x = jnp.where(pl.program_id(1) == 0, ctx_ref[...], lat_ref[...])
    xn = _rms(x, g_ref[...]) * (1.0 + mod[:, D_MODEL:]) + mod[:, :D_MODEL]
    xn = xn.astype(BF16)

    def proj(a, n):
        return _dot(xn, w_ref[:, a:a + n])

    zqkv = proj(P_QKV, Q_LORA + KV_LORA)
    rope = rope_ref[...]
    cos_q = rope[:, 0:LANES]
    sin_q = rope[:, LANES:2 * LANES]
    tab_k = rope[:, 2 * LANES:3 * LANES]
    qn = _rms(zqkv[:, :Q_LORA], qn_ref[...]).astype(BF16)
    q12 = _dot(qn, wq_ref[...])
    nq = N_HEADS * HEAD_PAD
    heads = []
    for h in range(N_HEADS):
        a = h * HEAD_PAD
        qh = q12[:, a:a + HEAD_PAD] * cos_q + q12[:, nq + a:nq + a + HEAD_PAD] * sin_q
        heads.append((qh * Q_SCALE).astype(BF16))
    q = jnp.concatenate(heads, axis=1)

    @pl.when(pl.program_id(1) == 0)
    def _():
        qc_out[...] = q

    @pl.when(pl.program_id(1) != 0)
    def _():
        ql_out[...] = q

    kvn = _rms(zqkv[:, Q_LORA:], kvn_ref[...]).astype(BF16)
    zpe = proj(P_PE, LANES)
    kpe = _dot((zpe * tab_k).astype(BF16), e_ref[...])
    kv = _dot(kvn, wkv_ref[...])
    k_out[...] = (kv[:, :nq] + kpe).astype(BF16)
    one_lane = lax.broadcasted_iota(jnp.int32, (1, nq), 1) % HEAD_PAD == V_DIM
    v_out[...] = (kv[:, nq:] + jnp.where(one_lane, 1.0, 0.0)).astype(BF16)
    cb_out[...] = proj(P_CB, CONV_WIDTH).astype(BF16)
    uc_out[...] = (proj(P_CC, CONV_WIDTH) * proj(P_CX, CONV_WIDTH)).astype(BF16)
    s5_out[...] = proj(P_S5, S5_WIDTH)
    gate_out[...] = _sigmoid(proj(P_G, N_BRANCH * D_MODEL)).astype(BF16)


def _token_specs(src, first_tile, width=D_MODEL):
    return [pl.BlockSpec((TM, width), lambda b, i: (b * src.ctx_stride, 0)),
            pl.BlockSpec((TM, width),
                         lambda b, i: (b * src.lat_stride + src.lat_off + jnp.maximum(i + first_tile - 1, 0), 0))]


def _inproj_call(src, mods, lw, rope, e_mat, nb, tps):
    t = nb * tps * TM
    rowb = lambda n: pl.BlockSpec((TM, n), lambda b, i: (b * tps + i, 0))
    nq = N_HEADS * HEAD_PAD
    outs = [(nq, BF16), (nq, BF16), (CONV_WIDTH, BF16), (CONV_WIDTH, BF16), (S5_WIDTH, F32),
            (N_BRANCH * D_MODEL, BF16)]
    q_specs = [pl.BlockSpec((TM, nq), lambda b, i: (b, 0)),
               pl.BlockSpec((TM, nq), lambda b, i: (b * (tps - 1) + jnp.maximum(i - 1, 0), 0))]
    q_shapes = [jax.ShapeDtypeStruct((nb * TM, nq), BF16), jax.ShapeDtypeStruct((nb * (tps - 1) * TM, nq), BF16)]
    return pl.pallas_call(
        _inproj_kernel,
        grid=(nb, tps),
        in_specs=_token_specs(src, 0) + [
                  pl.BlockSpec((1, 1, 2 * D_MODEL), lambda b, i: (jnp.where(i == 0, nb, b), 0, 0)),
                  _const_spec((1, D_MODEL)),
                  _const_spec((D_MODEL, P_COLS)),
                  _const_spec((1, Q_LORA)), _const_spec((1, KV_LORA)),
                  _const_spec((Q_LORA, 2 * nq)), _const_spec((KV_LORA, 2 * nq)),
                  _const_spec((LANES, nq)),
                  pl.BlockSpec((TM, 3 * LANES), lambda b, i: (i, 0))],
        out_specs=q_specs + [rowb(n) for n, _ in outs],
        out_shape=q_shapes + [jax.ShapeDtypeStruct((t, n), dt) for n, dt in outs],
        compiler_params=_cparams(2),
        name="in_projection",
    )(src.ctx, src.lat, mods, lw["norm_mix"], lw["w_in"], lw["q_norm"], lw["kv_norm"], lw["wq"], lw["wkv"],
      e_mat, rope)


def _attn_head(q, k, v):
    s = lax.dot_general(q, k, (((1,), (1,)), ((), ())), preferred_element_type=F32)
    p = jnp.exp2(s - jnp.max(s, axis=-1, keepdims=True)).astype(BF16)
    oe = _dot(p, v)
    return oe[:, :V_DIM] / oe[:, V_DIM:V_DIM + 1]


def _attn_kernel(q_ref, k_ref, v_ref, o_ref):
    outs = []
    for h in range(N_HEADS):
        sl = slice(h * HEAD_PAD, (h + 1) * HEAD_PAD)
        outs.append(_attn_head(q_ref[:, sl], k_ref[:, sl], v_ref[:, sl]))
    o_ref[...] = jnp.concatenate(outs, axis=-1).astype(BF16)


def _attn_call(q, k, v, nb, rows_q, rows_kv, kv_stride, name):
    nq = N_HEADS * HEAD_PAD
    nv = N_HEADS * V_DIM
    n_q = q.shape[0] // (nb * rows_q)
    kv_spec = pl.BlockSpec((rows_kv, nq), lambda b, i: (b * (kv_stride // rows_kv), 0), pipeline_mode=pl.Buffered(1))
    return pl.pallas_call(
        _attn_kernel,
        grid=(nb, n_q),
        in_specs=[pl.BlockSpec((rows_q, nq), lambda b, i: (b * n_q + i, 0)), kv_spec, kv_spec],
        out_specs=pl.BlockSpec((rows_q, nv), lambda b, i: (b * n_q + i, 0)),
        out_shape=jax.ShapeDtypeStruct((q.shape[0], nv), BF16),
        compiler_params=_cparams(2),
        name=name,
    )(q, k, v)


def _gelu_tanh(x):
    return 0.5 * x * (1.0 + jnp.tanh(math.sqrt(2.0 / math.pi) * (x + 0.044715 * (x * x * x))))


def _merge_kernel(ctx_ref, lat_ref, mod_ref, oc_ref, ol_ref, cb_ref, uc_ref, ucp_ref, ucn_ref, ys_ref, gate_ref,
                  wo_ref, cw_ref, cwo_ref, wglu_ref, wout_ref, out_ref, *, first_tile, tps):
    i = pl.program_id(1) + first_tile
    att = _dot(jnp.where(i == 0, oc_ref[...], ol_ref[...]), wo_ref[...])
    uc = uc_ref[...].astype(F32)
    prev_row = jnp.where(i >= 2, ucp_ref[HALO - 1:HALO, :].astype(F32), 0.0)
    next_row = jnp.where(jnp.logical_and(i >= 1, i < tps - 1), ucn_ref[0:1, :].astype(F32), 0.0)
    row = lax.broadcasted_iota(jnp.int32, (TM, 1), 0)
    up = jnp.where(row == 0, prev_row, pltpu.roll(uc, 1, axis=0))
    dn = jnp.where(row == TM - 1, next_row, pltpu.roll(uc, TM - 1, axis=0))
    y = up * cw_ref[0:1, :] + uc * cw_ref[1:2, :] + dn * cw_ref[2:3, :]
    conv = _dot((cb_ref[...].astype(F32) * y).astype(BF16), cwo_ref[...])
    z = _dot(_gelu_tanh(ys_ref[...]).astype(BF16), wglu_ref[...])
    s5o = z[:, :D_MODEL] * _sigmoid(z[:, D_MODEL:])
    g = gate_ref[...].astype(F32)
    merged = g[:, :D_MODEL] * att + g[:, D_MODEL:2 * D_MODEL] * conv + g[:, 2 * D_MODEL:] * s5o
    x = jnp.where(i == 0, ctx_ref[...], lat_ref[...])
    out_ref[...] = x + mod_ref[0] * _dot(merged.astype(BF16), wout_ref[...])


def _merge_call(src, mods, o_src, cb, uc, ys, gate, lw, nb, tps, first_tile):
    t = cb.shape[0]
    blk = lambda b, i: b * tps + i + first_tile
    rowb = lambda n: pl.BlockSpec((TM, n), lambda b, i: (blk(b, i), 0))
    per_halo = TM // HALO
    last_halo = t // HALO - 1
    return pl.pallas_call(
        functools.partial(_merge_kernel, first_tile=first_tile, tps=tps),
        grid=(nb, tps - first_tile),
        in_specs=_token_specs(src, first_tile) + [
                  pl.BlockSpec((1, 1, D_MODEL), lambda b, i: (jnp.where(i + first_tile == 0, nb, b), 0, 2))]
                 + _token_specs(o_src, first_tile, N_HEADS * V_DIM) + [
                  rowb(CONV_WIDTH), rowb(CONV_WIDTH),
                  pl.BlockSpec((HALO, CONV_WIDTH), lambda b, i: (jnp.maximum(blk(b, i) * per_halo - 1, 0), 0)),
                  pl.BlockSpec((HALO, CONV_WIDTH),
                               lambda b, i: (jnp.minimum((blk(b, i) + 1) * per_halo, last_halo), 0)),
                  rowb(S5_WIDTH), rowb(N_BRANCH * D_MODEL),
                  _const_spec((N_HEADS * V_DIM, D_MODEL)), _const_spec((CONV_K, CONV_WIDTH)),
                  _const_spec((CONV_WIDTH, D_MODEL)),
                  _const_spec((S5_WIDTH, 2 * D_MODEL)), _const_spec((D_MODEL, D_MODEL))],
        out_specs=rowb(D_MODEL),
        out_shape=jax.ShapeDtypeStruct((t, D_MODEL), F32),
        compiler_params=_cparams(2),
        name="branch_merge",
    )(src.ctx, src.lat, mods, o_src.ctx, o_src.lat, cb, uc, uc, uc, ys, gate,
      lw["w_o"], lw["conv_w"], lw["conv_w_out"], lw["w_glu"], lw["w_out"])


def _mlp_kernel(x_ref, mod_ref, g_ref, w1_ref, w2_ref, gf_ref, out_ref, *, final):
    x = x_ref[...]
    mod = mod_ref[0]
    h = (_rms(x, g_ref[...]) * (1.0 + mod[:, D_MODEL:2 * D_MODEL]) + mod[:, :D_MODEL]).astype(BF16)
    acc = jnp.zeros((TM, D_MODEL), F32)
    for c in range(D_FF // D_MODEL):
        a = jnp.maximum(_dot(h, w1_ref[:, c * D_MODEL:(c + 1) * D_MODEL]), 0.0)
        acc = acc + _dot((a * a).astype(BF16), w2_ref[c * D_MODEL:(c + 1) * D_MODEL, :])
    y = x + mod[:, 2 * D_MODEL:] * acc
    if final:
        y = _rms(y, gf_ref[...])
    out_ref[...] = y


def _mlp_call(x1, mods, lw, norm_final, nb, tps, first_tile, final):
    n_tiles = tps - first_tile
    return pl.pallas_call(
        functools.partial(_mlp_kernel, final=final),
        grid=(nb, n_tiles),
        in_specs=[pl.BlockSpec((TM, D_MODEL), lambda b, i: (b * tps + i + first_tile, 0)),
                  pl.BlockSpec((1, 1, 3 * D_MODEL), lambda b, i: (jnp.where(i + first_tile == 0, nb, b), 0, 1)),
                  _const_spec((1, D_MODEL)),
                  _const_spec((D_MODEL, D_FF)), _const_spec((D_FF, D_MODEL)),
                  _const_spec((1, D_MODEL))],
        out_specs=pl.BlockSpec((TM, D_MODEL), lambda b, i: (b * n_tiles + i, 0)),
        out_shape=jax.ShapeDtypeStruct((nb * n_tiles * TM, D_MODEL), F32),
        compiler_params=_cparams(2),
        name="mlp",
    )(x1, mods, lw["norm_mlp"], lw["w1"], lw["w2"], norm_final)


def _rot_half(w):
    wr = w.reshape(w.shape[:-1] + (2, 2, QK_ROPE // 4))
    return jnp.concatenate([-wr[..., 1:, :], wr[..., :1, :]], axis=-2).reshape(w.shape)


def _layer_weights(i, w_in, norm_mix, q_norm, w_uq, kv_norm, w_ukv, w_o, conv_w, conv_w_out, s5_d, w_glu,
                   w_out, norm_mlp, w1, w2):
    wi = w_in[i]
    pe = wi[:, OFF_PE:OFF_CB]
    w_in_p = jnp.concatenate(
        [wi[:, OFF_Q:OFF_PE], wi[:, OFF_CB:OFF_G], wi[:, OFF_G:], pe, _rot_half(pe),
         jnp.zeros((D_MODEL, LANES - 2 * QK_ROPE), F32)], axis=1).astype(BF16)
    uq = w_uq[i].reshape(Q_LORA, N_HEADS, QK_NOPE + QK_ROPE)
    zpad = jnp.zeros((Q_LORA, N_HEADS, HEAD_PAD - QK_NOPE - QK_ROPE), F32)
    wq1 = jnp.concatenate([uq, zpad], axis=-1).reshape(Q_LORA, N_HEADS * HEAD_PAD)
    wq2 = jnp.concatenate([jnp.zeros((Q_LORA, N_HEADS, QK_NOPE), F32), _rot_half(uq[..., QK_NOPE:]), zpad],
                          axis=-1).reshape(Q_LORA, N_HEADS * HEAD_PAD)
    ukv = w_ukv[i].reshape(KV_LORA, N_HEADS, QK_NOPE + V_DIM)
    wk = jnp.concatenate([ukv[..., :QK_NOPE], jnp.zeros((KV_LORA, N_HEADS, HEAD_PAD - QK_NOPE), F32)],
                         axis=-1).reshape(KV_LORA, N_HEADS * HEAD_PAD)
    wv = jnp.concatenate([ukv[..., QK_NOPE:], jnp.zeros((KV_LORA, N_HEADS, HEAD_PAD - V_DIM), F32)],
                         axis=-1).reshape(KV_LORA, N_HEADS * HEAD_PAD)
    return dict(
        w_in=w_in_p, norm_mix=norm_mix[i].reshape(1, D_MODEL),
        q_norm=q_norm[i].reshape(1, Q_LORA), kv_norm=kv_norm[i].reshape(1, KV_LORA),
        wq=jnp.concatenate([wq1, wq2], axis=1).astype(BF16),
        wkv=jnp.concatenate([wk, wv], axis=1).astype(BF16),
        w_o=w_o[i].astype(BF16), conv_w=conv_w[i], conv_w_out=conv_w_out[i].astype(BF16),
        s5_d=s5_d[i].reshape(1, S5_WIDTH), w_glu=w_glu[i].astype(BF16), w_out=w_out[i].astype(BF16),
        norm_mlp=norm_mlp[i].reshape(1, D_MODEL), w1=w1[i].astype(BF16), w2=w2[i].astype(BF16))


def _rope_tables(n_ctx, n_tokens):
    rows = n_tokens // GRID_W
    pos = jnp.stack([jnp.repeat(jnp.arange(rows), GRID_W), jnp.tile(jnp.arange(GRID_W), rows)], -1).astype(F32)
    n_freq = QK_ROPE // 4
    inv = ROPE_THETA ** (-jnp.arange(n_freq, dtype=F32) / n_freq)
    ang = pos[:, :, None, None] * inv[None, None, None, :]
    ang = jnp.broadcast_to(ang, (n_tokens, 2, 2, n_freq)).reshape(n_tokens, QK_ROPE)
    cos = jnp.concatenate([jnp.ones((n_ctx, QK_ROPE), F32), jnp.cos(ang)], axis=0)
    sin = jnp.concatenate([jnp.zeros((n_ctx, QK_ROPE), F32), jnp.sin(ang)], axis=0)
    s = n_ctx + n_tokens
    one = jnp.ones((s, QK_NOPE), F32)
    z = lambda n: jnp.zeros((s, n), F32)
    pad = HEAD_PAD - QK_NOPE - QK_ROPE
    return jnp.concatenate([one, cos, z(pad), z(QK_NOPE), sin, z(pad), cos, sin, z(LANES - 2 * QK_ROPE)], axis=1)


def _pe_spread():
    j = jnp.arange(QK_ROPE)
    e = jnp.zeros((LANES, N_HEADS * HEAD_PAD), F32)
    for h in range(N_HEADS):
        e = e.at[j, h * HEAD_PAD + QK_NOPE + j].set(1.0)
        e = e.at[QK_ROPE + j, h * HEAD_PAD + QK_NOPE + j].set(1.0)
    return e.astype(BF16)


def kernel(x, c, ctx, c_ctx, ada_w, ada_b, norm_mix, w_in, mla_q_norm, mla_w_uq, mla_kv_norm, mla_w_ukv, mla_w_o, conv_w, conv_w_out, s5_a_re, s5_a_im, s5_log_dt, s5_b_re, s5_b_im, s5_c_re, s5_c_im, s5_d, s5_w_glu, w_out, norm_mlp, mlp_w1, mlp_w2, norm_final):
    nb, n_lat, _ = x.shape
    n_ctx = ctx.shape[1]
    depth = ada_w.shape[0]
    assert n_ctx == TM and n_lat % TQ == 0 and n_lat % GRID_W == 0 and nb == SUBLANES
    s = n_ctx + n_lat
    tps = s // TM
    t = nb * s
    n_chunks = s // S5_CHUNK
    n_ctx_chunks = n_ctx // S5_CHUNK

    c16 = jnp.zeros((16, D_MODEL), F32).at[:nb].set(c).at[nb].set(c_ctx)
    mods = _ada_call(c16, ada_w, ada_b)[:, :nb + 1].reshape(depth, nb + 1, 1, N_MOD * D_MODEL)
    ops = _s5_operators(s5_a_re, s5_a_im, s5_log_dt, s5_b_re, s5_b_im, s5_c_re, s5_c_im)
    rope = _rope_tables(n_ctx, n_lat)
    e_mat = _pe_spread()
    src = _TokenSource(ctx.reshape(nb * n_ctx, D_MODEL), x.reshape(nb * n_lat, D_MODEL), 1, tps - 1, 0)
    gf = norm_final.reshape(1, D_MODEL)

    for i in range(depth):
        last = i == depth - 1
        ft = 1 if last else 0
        lw = _layer_weights(i, w_in, norm_mix, mla_q_norm, mla_w_uq, mla_kv_norm, mla_w_ukv, mla_w_o, conv_w,
                            conv_w_out, s5_d, s5_w_glu, w_out, norm_mlp, mlp_w1, mlp_w2)
        q_ctx, q_lat, k, v, cb, uc, s5u, gate = _inproj_call(src, mods[i], lw, rope, e_mat, nb, tps)
        o_lat = _attn_call(q_lat, k, v, nb, TQ, s, s, "attention")
        o_ctx = o_lat if last else _attn_call(q_ctx, k, v, nb, TM, n_ctx, s, "attention_ctx")
        o_src = _TokenSource(o_ctx, o_lat, 1, tps - 1, 0)
        ys = _s5_call(s5u, ops, lw["s5_d"], i, nb, n_chunks, n_ctx_chunks)
        x1 = _merge_call(src, mods[i], o_src, cb, uc, ys, gate, lw, nb, tps, ft)
        xs = _mlp_call(x1, mods[i], lw, gf, nb, tps, ft, last)
        src = _TokenSource(xs, xs, tps, tps, 1)
    return xs.reshape(nb, n_lat, D_MODEL)
```

```python
import functools
import math
from typing import NamedTuple

import jax
import jax.numpy as jnp
from jax import lax
from jax.experimental import pallas as pl
from jax.experimental.pallas import tpu as pltpu

F32 = jnp.float32
BF16 = jnp.bfloat16

D_MODEL = 1024
GRID_W = 64
N_HEADS = 8
QK_NOPE = 64
QK_ROPE = 32
V_DIM = 64
Q_LORA = 256
KV_LORA = 256
ROPE_THETA = 10000.0
ATTN_SCALE = 1.0 / math.sqrt(QK_NOPE + QK_ROPE)
Q_SCALE = ATTN_SCALE * math.log2(math.e)
CONV_WIDTH = 512
CONV_K = 3
S5_WIDTH = 512
S5_GROUP = 16
S5_GROUPS = S5_WIDTH // S5_GROUP
S5_STATE = 64
N_BRANCH = 3
D_FF = 4 * D_MODEL
EPS = 1e-6
N_MOD = 6

OFF_Q = 0
OFF_KV = OFF_Q + Q_LORA
OFF_PE = OFF_KV + KV_LORA
OFF_CB = OFF_PE + QK_ROPE
OFF_CC = OFF_CB + CONV_WIDTH
OFF_CX = OFF_CC + CONV_WIDTH
OFF_S5 = OFF_CX + CONV_WIDTH
OFF_G = OFF_S5 + S5_WIDTH
IN_COLS = OFF_G + N_BRANCH * D_MODEL

LANES = 128
SUBLANES = 8
HEAD_PAD = LANES

P_QKV = 0
P_CB = P_QKV + Q_LORA + KV_LORA
P_CC = P_CB + CONV_WIDTH
P_CX = P_CC + CONV_WIDTH
P_S5 = P_CX + CONV_WIDTH
P_G = P_S5 + S5_WIDTH
P_PE = P_G + N_BRANCH * D_MODEL
P_COLS = P_PE + LANES

TM = 256
TQ = 512
HALO = 16
S5_CHUNK = 16
S5_ROW = S5_CHUNK * S5_GROUP
GROUPS_PER_SLAB = LANES // S5_GROUP
S5_SLABS = S5_WIDTH // LANES
VMEM_LIMIT = 56 * 1024 * 1024


class _TokenSource(NamedTuple):
    ctx: jax.Array
    lat: jax.Array
    ctx_stride: int
    lat_stride: int
    lat_off: int


def _cparams(n_axes):
    return pltpu.CompilerParams(dimension_semantics=("arbitrary",) * n_axes,
                                vmem_limit_bytes=VMEM_LIMIT)


def _const_spec(shape):
    nd = len(shape)
    return pl.BlockSpec(shape, lambda *_: (0,) * nd, pipeline_mode=pl.Buffered(1))


def _rms(x, g):
    return x * lax.rsqrt(jnp.mean(x * x, axis=-1, keepdims=True) + EPS) * g


def _sigmoid(x):
    return 1.0 / (1.0 + jnp.exp(-x))


def _dot(a, b):
    return jnp.dot(a, b, preferred_element_type=F32)


def _ada_kernel(c_ref, w_ref, b_ref, o_ref):
    c = c_ref[...]
    s = (c * _sigmoid(c)).astype(BF16)
    o_ref[0] = _dot(s, w_ref[0].astype(BF16)) + b_ref[0]


def _ada_call(c16, ada_w, ada_b):
    depth = ada_w.shape[0]
    n_col = N_MOD * D_MODEL
    bn = n_col // 4
    return pl.pallas_call(
        _ada_kernel,
        grid=(depth, n_col // bn),
        in_specs=[pl.BlockSpec((16, D_MODEL), lambda l, j: (0, 0)),
                  pl.BlockSpec((1, D_MODEL, bn), lambda l, j: (l, 0, j)),
                  pl.BlockSpec((1, 1, bn), lambda l, j: (l, 0, j))],
        out_specs=pl.BlockSpec((1, 16, bn), lambda l, j: (l, 0, j)),
        out_shape=jax.ShapeDtypeStruct((depth, 16, n_col), F32),
        compiler_params=_cparams(2),
        name="ada_mod",
    )(c16, ada_w, ada_b.reshape(depth, 1, n_col))


def _shift_blocks(x, m):
    lo, hi = x[:, :LANES], x[:, LANES:]
    lane = lax.broadcasted_iota(jnp.int32, lo.shape, 1)
    zero = jnp.zeros_like(lo)
    if m == 0:
        return x
    k = abs(m) % GROUPS_PER_SLAB
    if m > 0:
        cut = S5_GROUP * k
        if m < GROUPS_PER_SLAB:
            r_lo, r_hi = pltpu.roll(lo, cut, axis=1), pltpu.roll(hi, cut, axis=1)
            out = [jnp.where(lane >= cut, r_lo, zero), jnp.where(lane >= cut, r_hi, r_lo)]
        elif k == 0:
            out = [zero, lo]
        else:
            out = [zero, jnp.where(lane >= cut, pltpu.roll(lo, cut, axis=1), zero)]
    else:
        cut = LANES - S5_GROUP * k
        if -m < GROUPS_PER_SLAB:
            r_lo, r_hi = pltpu.roll(lo, cut, axis=1), pltpu.roll(hi, cut, axis=1)
            out = [jnp.where(lane < cut, r_lo, r_hi), jnp.where(lane < cut, r_hi, zero)]
        elif k == 0:
            out = [hi, zero]
        else:
            out = [jnp.where(lane < cut, pltpu.roll(hi, cut, axis=1), zero), zero]
    return jnp.concatenate(out, axis=1)


def _s5prep_kernel(are_ref, aim_ref, ldt_ref, bre_ref, bim_ref, cre_ref, cim_ref,
                   w1_ref, cmre_ref, cmim_ref, a16re_ref, a16im_ref):
    ck = S5_CHUNK
    rows = lax.broadcasted_iota(jnp.int32, (S5_ROW, 1), 0) // S5_GROUP
    fwd = lax.broadcasted_iota(jnp.int32, (1, 2 * S5_STATE), 1) < S5_STATE
    dn = (((1,), (1,)), ((), ()))
    hi = lax.Precision.HIGHEST
    for g in range(GROUPS_PER_SLAB):
        are = are_ref[0, g:g + 1, :]
        aim = aim_ref[0, g:g + 1, :]
        dt = jnp.exp(ldt_ref[0, g:g + 1, :])

        def power(p):
            m = jnp.exp(p * (dt * are))
            ang = p * (dt * aim)
            return m * jnp.cos(ang), m * jnp.sin(ang)

        abre, abim = power(1.0)
        den = are * are + aim * aim
        nr = abre - 1.0
        fre = (nr * are + abim * aim) / den
        fim = (abim * are - nr * aim) / den
        b_re, b_im = bre_ref[0, g], bim_ref[0, g]
        bb_re = fre * b_re - fim * b_im
        bb_im = fre * b_im + fim * b_re
        tile = lambda t: jnp.concatenate([t] * ck, axis=0)
        step = SUBLANES * (rows // SUBLANES) + ((rows % SUBLANES - g) & (SUBLANES - 1))
        stepf = step.astype(F32)
        p_re, p_im = power(jnp.where(fwd, float(ck - 1) - stepf, stepf))
        t_re, t_im = tile(bb_re), tile(bb_im)
        w1_ref[0, g, :, S5_ROW:S5_ROW + 2 * S5_STATE] = (p_re * t_re - p_im * t_im).astype(BF16)
        w1_ref[0, g, :, S5_ROW + 2 * S5_STATE:] = (p_re * t_im + p_im * t_re).astype(BF16)
        c_re, c_im = tile(cre_ref[0, g]), tile(cim_ref[0, g])
        p_re, p_im = power(jnp.where(fwd, stepf + 1.0, float(ck) - stepf))
        cmre_ref[0, g] = (c_re * p_re - c_im * p_im).T.astype(BF16)
        cmim_ref[0, g] = (-(c_re * p_im + c_im * p_re)).T.astype(BF16)
        lag = rows.astype(F32)
        p_re, p_im = power(jnp.where(fwd, lag, float(ck - 1) - lag))
        x_re = p_re * c_re - p_im * c_im
        x_im = p_re * c_im + p_im * c_re

        def kernels(mask):
            return (lax.dot_general(jnp.where(mask, bb_re, 0.0), x_re, dn, precision=hi, preferred_element_type=F32)
                    - lax.dot_general(jnp.where(mask, bb_im, 0.0), x_im, dn, precision=hi,
                                      preferred_element_type=F32))

        kf = kernels(fwd)
        kb = kernels(jnp.logical_not(fwd))
        for pos in range(ck):
            s = SUBLANES * (pos // SUBLANES) + (pos % SUBLANES - g) % SUBLANES
            blockrow = _shift_blocks(kf, s) + _shift_blocks(kb, s - (ck - 1))
            halves = [blockrow[:, :LANES], blockrow[:, LANES:]]
            if g:
                halves = [pltpu.roll(t, S5_GROUP * g, axis=1) for t in halves]
            w1_ref[0, g, pos * S5_GROUP:(pos + 1) * S5_GROUP, :S5_ROW] = jnp.concatenate(halves, axis=1).astype(BF16)
        e16 = jnp.exp(float(ck) * (dt * are))
        a16re_ref[0, g:g + 1, :] = e16 * jnp.cos(float(ck) * (dt * aim))
        a16im_ref[0, g:g + 1, :] = e16 * jnp.sin(float(ck) * (dt * aim))


def _s5_operators(a_re, a_im, log_dt, b_re, b_im, c_re, c_im):
    depth = a_re.shape[0]
    n = depth * S5_SLABS
    gl = GROUPS_PER_SLAB
    vec = lambda t: jnp.moveaxis(t, 1, 2).reshape(n, gl, 2 * S5_STATE)
    ldt = jnp.broadcast_to(jnp.moveaxis(log_dt, 1, 2)[..., None], (depth, S5_GROUPS, 2, S5_STATE))
    bmat = lambda t: jnp.transpose(t, (0, 2, 4, 1, 3)).reshape(n, gl, S5_GROUP, 2 * S5_STATE)
    cmat = lambda t: jnp.transpose(t, (0, 2, 3, 1, 4)).reshape(n, gl, S5_GROUP, 2 * S5_STATE)
    vspec = pl.BlockSpec((1, gl, 2 * S5_STATE), lambda m: (m, 0, 0))
    mspec = pl.BlockSpec((1, gl, S5_GROUP, 2 * S5_STATE), lambda m: (m, 0, 0, 0))
    w1, cre, cim, a16re, a16im = pl.pallas_call(
        _s5prep_kernel,
        grid=(n,),
        in_specs=[vspec, vspec, vspec, mspec, mspec, mspec, mspec],
        out_specs=[pl.BlockSpec((1, gl, S5_ROW, S5_ROW + 4 * S5_STATE), lambda m: (m, 0, 0, 0)),
                   pl.BlockSpec((1, gl, 2 * S5_STATE, S5_ROW), lambda m: (m, 0, 0, 0)),
                   pl.BlockSpec((1, gl, 2 * S5_STATE, S5_ROW), lambda m: (m, 0, 0, 0)), vspec, vspec],
        out_shape=[jax.ShapeDtypeStruct((n, gl, S5_ROW, S5_ROW + 4 * S5_STATE), BF16),
                   jax.ShapeDtypeStruct((n, gl, 2 * S5_STATE, S5_ROW), BF16),
                   jax.ShapeDtypeStruct((n, gl, 2 * S5_STATE, S5_ROW), BF16),
                   jax.ShapeDtypeStruct((n, gl, 2 * S5_STATE), F32), jax.ShapeDtypeStruct((n, gl, 2 * S5_STATE), F32)],
        compiler_params=_cparams(1),
        name="s5_operators",
    )(vec(a_re), vec(a_im), ldt.reshape(n, gl, 2 * S5_STATE), bmat(b_re), bmat(b_im), cmat(c_re), cmat(c_im))
    r = lambda t: t.reshape((depth, S5_SLABS) + t.shape[1:])
    return dict(w1=r(w1), cre=r(cre), cim=r(cim), are=r(a16re), aim=r(a16im))


def _s5_kernel(z_ref, w1_ref, cre_ref, cim_ref, are_ref, aim_ref, d_ref, y_ref,
               yin, lre, lim, sre_f, sre_b, sim_f, sim_b, *, n_chunks, n_ctx_chunks, pitch):
    ng = GROUPS_PER_SLAB
    blk = lax.broadcasted_iota(jnp.int32, (n_chunks, LANES), 1) // S5_GROUP

    def step_rows(j):
        return z_ref[pl.ds(j, n_chunks, stride=S5_CHUNK), :]

    rot = [[step_rows(SUBLANES * h + s) if s == 0
            else pltpu.roll(step_rows(SUBLANES * h + s), S5_GROUP * s, axis=1)
            for s in range(SUBLANES)] for h in range(2)]
    for g in range(ng):
        halves = []
        for h in range(2):
            u = rot[h][(0 - g) % ng]
            for q in range(1, ng):
                u = jnp.where(blk == q, rot[h][(q - g) % ng], u)
            halves.append(u)
        r = _dot(jnp.concatenate(halves, axis=1).astype(BF16), w1_ref[g])
        yin[g] = r[:, :S5_ROW]
        lre[pl.ds(g * pitch, n_chunks), :] = r[:, S5_ROW:S5_ROW + 2 * S5_STATE]
        lim[pl.ds(g * pitch, n_chunks), :] = r[:, S5_ROW + 2 * S5_STATE:]

    ar = are_ref[...]
    ai = aim_ref[...]
    fwd_lane = lax.broadcasted_iota(jnp.int32, (ng, 2 * S5_STATE), 1) < S5_STATE

    def body(kstep, carry):
        st_re, st_im = carry
        rb = jnp.where(kstep < n_ctx_chunks, n_ctx_chunks - 1 - kstep, n_chunks - 1 + n_ctx_chunks - kstep)
        rows_f = pl.ds(kstep, ng, stride=pitch)
        rows_b = pl.ds(rb, ng, stride=pitch)
        sre_f[rows_f, :] = st_re
        sre_b[rows_b, :] = st_re
        sim_f[rows_f, :] = st_im
        sim_b[rows_b, :] = st_im
        loc_re = jnp.where(fwd_lane, lre[rows_f, :], lre[rows_b, :])
        loc_im = jnp.where(fwd_lane, lim[rows_f, :], lim[rows_b, :])
        return ar * st_re - ai * st_im + loc_re, ar * st_im + ai * st_re + loc_im

    zero = jnp.zeros((ng, 2 * S5_STATE), F32)
    lax.fori_loop(0, n_chunks, body, (zero, zero))

    fwd_big = lax.broadcasted_iota(jnp.int32, (n_chunks, 2 * S5_STATE), 1) < S5_STATE
    for g in range(ng):
        rows = pl.ds(g * pitch, n_chunks)
        s_re = jnp.where(fwd_big, sre_f[rows, :], sre_b[rows, :]).astype(BF16)
        s_im = jnp.where(fwd_big, sim_f[rows, :], sim_b[rows, :]).astype(BF16)
        yin[g] = yin[g] + _dot(s_re, cre_ref[g]) + _dot(s_im, cim_ref[g])

    d = d_ref[...]
    for h in range(2):
        for s in range(SUBLANES):
            v = yin[(0 - s) % ng, :, h * LANES:(h + 1) * LANES]
            for q in range(1, ng):
                v = jnp.where(blk == q, yin[(q - s) % ng, :, h * LANES:(h + 1) * LANES], v)
            if s:
                v = pltpu.roll(v, LANES - S5_GROUP * s, axis=1)
            j = SUBLANES * h + s
            y_ref[pl.ds(j, n_chunks, stride=S5_CHUNK), :] = v + d * step_rows(j)


def _s5_call(s5u, ops, s5_d, layer, nb, n_chunks, n_ctx_chunks):
    t = s5u.shape[0]
    s = t // nb
    pitch = -(-n_chunks // SUBLANES) * SUBLANES
    if (pitch // SUBLANES) % 2 == 0:
        pitch += SUBLANES
    gl = GROUPS_PER_SLAB
    mat = lambda r, c: pl.BlockSpec((None, None, gl, r, c), lambda a, b: (layer, a, 0, 0, 0))
    vec = pl.BlockSpec((None, None, gl, 2 * S5_STATE), lambda a, b: (layer, a, 0, 0))
    state = pltpu.VMEM((gl * pitch, 2 * S5_STATE), F32)
    return pl.pallas_call(
        functools.partial(_s5_kernel, n_chunks=n_chunks, n_ctx_chunks=n_ctx_chunks, pitch=pitch),
        grid=(S5_SLABS, nb),
        in_specs=[pl.BlockSpec((s, LANES), lambda a, b: (b, a)),
                  mat(S5_ROW, S5_ROW + 4 * S5_STATE), mat(2 * S5_STATE, S5_ROW), mat(2 * S5_STATE, S5_ROW),
                  vec, vec, pl.BlockSpec((1, LANES), lambda a, b: (0, a))],
        out_specs=pl.BlockSpec((s, LANES), lambda a, b: (b, a)),
        out_shape=jax.ShapeDtypeStruct((t, S5_WIDTH), F32),
        scratch_shapes=[pltpu.VMEM((gl, n_chunks, S5_ROW), F32)] + [state] * 6,
        compiler_params=_cparams(2),
        name="s5_scan",
    )(s5u, ops["w1"], ops["cre"], ops["cim"], ops["are"], ops["aim"], s5_d)


def _inproj_kernel(ctx_ref, lat_ref, mod_ref, g_ref, w_ref, qn_ref, kvn_ref, wq_ref, wkt_ref, wv_ref, et_ref, rope_ref,
                   qc_out, ql_out, kt_out, v_out, cb_out, uc_out, s5_out, gate_out):
    mod = mod_ref[0]
    x = jnp.where(pl.program_id(1) == 0, ctx_ref[...], lat_ref[...])
    xn = _rms(x, g_ref[...]) * (1.0 + mod[:, D_MODEL:]) + mod[:, :D_MODEL]
    xn = xn.astype(BF16)

    def proj(a, n):
        return _dot(xn, w_ref[:, a:a + n])

    zqkv = proj(P_QKV, Q_LORA + KV_LORA)
    rope = rope_ref[...]
    cos_q = rope[:, 0:LANES]
    sin_q = rope[:, LANES:2 * LANES]
    tab_k = rope[:, 2 * LANES:3 * LANES]
    qn = _rms(zqkv[:, :Q_LORA], qn_ref[...]).astype(BF16)
    q12 = _dot(qn, wq_ref[...])
    nq = N_HEADS * HEAD_PAD
    heads = []
    for h in range(N_HEADS):
        a = h * HEAD_PAD
        qh = q12[:, a:a + HEAD_PAD] * cos_q + q12[:, nq + a:nq + a + HEAD_PAD] * sin_q
        heads.append((qh * Q_SCALE).astype(BF16))
    q = jnp.concatenate(heads, axis=1)

    @pl.when(pl.program_id(1) == 0)
    def _():
        qc_out[...] = q

    @pl.when(pl.program_id(1) != 0)
    def _():
        ql_out[...] = q

    kvn = _rms(zqkv[:, Q_LORA:], kvn_ref[...]).astype(BF16)
    zpe = proj(P_PE, LANES)
    nt = (((1,), (1,)), ((), ()))
    kt = (lax.dot_general(wkt_ref[...], kvn, nt, preferred_element_type=F32)
          + lax.dot_general(et_ref[...], (zpe * tab_k).astype(BF16), nt, preferred_element_type=F32))
    kt_out[...] = kt.astype(BF16)
    one_lane = lax.broadcasted_iota(jnp.int32, (1, nq), 1) % HEAD_PAD == V_DIM
    v_out[...] = (_dot(kvn, wv_ref[...]) + jnp.where(one_lane, 1.0, 0.0)).astype(BF16)
    cb_out[...] = proj(P_CB, CONV_WIDTH).astype(BF16)
    uc_out[...] = (proj(P_CC, CONV_WIDTH) * proj(P_CX, CONV_WIDTH)).astype(BF16)
    s5_out[...] = proj(P_S5, S5_WIDTH)
    gate_out[...] = _sigmoid(proj(P_G, N_BRANCH * D_MODEL)).astype(BF16)


def _token_specs(src, first_tile, width=D_MODEL):
    return [pl.BlockSpec((TM, width), lambda b, i: (b * src.ctx_stride, 0)),
            pl.BlockSpec((TM, width),
                         lambda b, i: (b * src.lat_stride + src.lat_off + jnp.maximum(i + first_tile - 1, 0), 0))]


def _inproj_call(src, mods, lw, rope, e_mat, nb, tps):
    t = nb * tps * TM
    rowb = lambda n: pl.BlockSpec((TM, n), lambda b, i: (b * tps + i, 0))
    nq = N_HEADS * HEAD_PAD
    outs = [(nq, BF16), (CONV_WIDTH, BF16), (CONV_WIDTH, BF16), (S5_WIDTH, F32), (N_BRANCH * D_MODEL, BF16)]
    q_specs = [pl.BlockSpec((TM, nq), lambda b, i: (b, 0)),
               pl.BlockSpec((TM, nq), lambda b, i: (b * (tps - 1) + jnp.maximum(i - 1, 0), 0))]
    q_shapes = [jax.ShapeDtypeStruct((nb * TM, nq), BF16), jax.ShapeDtypeStruct((nb * (tps - 1) * TM, nq), BF16)]
    return pl.pallas_call(
        _inproj_kernel,
        grid=(nb, tps),
        in_specs=_token_specs(src, 0) + [
                  pl.BlockSpec((1, 1, 2 * D_MODEL), lambda b, i: (jnp.where(i == 0, nb, b), 0, 0)),
                  _const_spec((1, D_MODEL)),
                  _const_spec((D_MODEL, P_COLS)),
                  _const_spec((1, Q_LORA)), _const_spec((1, KV_LORA)),
                  _const_spec((Q_LORA, 2 * nq)), _const_spec((nq, KV_LORA)), _const_spec((KV_LORA, nq)),
                  _const_spec((nq, LANES)),
                  pl.BlockSpec((TM, 3 * LANES), lambda b, i: (i, 0))],
        out_specs=q_specs + [pl.BlockSpec((nq, TM), lambda b, i: (b, i))] + [rowb(n) for n, _ in outs],
        out_shape=q_shapes + [jax.ShapeDtypeStruct((nb * nq, tps * TM), BF16)]
        + [jax.ShapeDtypeStruct((t, n), dt) for n, dt in outs],
        compiler_params=_cparams(2),
        name="in_projection",
    )(src.ctx, src.lat, mods, lw["norm_mix"], lw["w_in"], lw["q_norm"], lw["kv_norm"], lw["wq"], lw["wkt"],
      lw["wv"], e_mat.T, rope)


def _attn_head(q, kt, v):
    s = _dot(q, kt)
    p = jnp.exp2(s - jnp.max(s, axis=-1, keepdims=True)).astype(BF16)
    oe = _dot(p, v)
    return oe[:, :V_DIM] / oe[:, V_DIM:V_DIM + 1]


def _attn_kernel(q_ref, kt_ref, v_ref, o_ref):
    outs = []
    for h in range(N_HEADS):
        sl = slice(h * HEAD_PAD, (h + 1) * HEAD_PAD)
        outs.append(_attn_head(q_ref[:, sl], kt_ref[sl, :], v_ref[:, sl]))
    o_ref[...] = jnp.concatenate(outs, axis=-1).astype(BF16)


def _attn_call(q, kt, v, nb, rows_q, rows_kv, kv_stride, name):
    nq = N_HEADS * HEAD_PAD
    nv = N_HEADS * V_DIM
    n_q = q.shape[0] // (nb * rows_q)
    v_spec = pl.BlockSpec((rows_kv, nq), lambda b, i: (b * (kv_stride // rows_kv), 0), pipeline_mode=pl.Buffered(1))
    kt_spec = pl.BlockSpec((nq, rows_kv), lambda b, i: (b, 0), pipeline_mode=pl.Buffered(1))
    return pl.pallas_call(
        _attn_kernel,
        grid=(nb, n_q),
        in_specs=[pl.BlockSpec((rows_q, nq), lambda b, i: (b * n_q + i, 0)), kt_spec, v_spec],
        out_specs=pl.BlockSpec((rows_q, nv), lambda b, i: (b * n_q + i, 0)),
        out_shape=jax.ShapeDtypeStruct((q.shape[0], nv), BF16),
        compiler_params=_cparams(2),
        name=name,
    )(q, kt, v)


def _gelu_tanh(x):
    return 0.5 * x * (1.0 + jnp.tanh(math.sqrt(2.0 / math.pi) * (x + 0.044715 * (x * x * x))))


def _merge_mlp_kernel(ctx_ref, lat_ref, moda_ref, modm_ref, oc_ref, ol_ref, cb_ref, uc_ref, ucp_ref, ucn_ref, ys_ref,
                      gate_ref, wo_ref, cw_ref, cwo_ref, wglu_ref, wout_ref, gm_ref, w1_ref, w2_ref, gf_ref, out_ref,
                      *, first_tile, tps, final):
    i = pl.program_id(1) + first_tile
    att = _dot(jnp.where(i == 0, oc_ref[...], ol_ref[...]), wo_ref[...])
    uc = uc_ref[...].astype(F32)
    prev_row = jnp.where(i >= 2, ucp_ref[HALO - 1:HALO, :].astype(F32), 0.0)
    next_row = jnp.where(jnp.logical_and(i >= 1, i < tps - 1), ucn_ref[0:1, :].astype(F32), 0.0)
    row = lax.broadcasted_iota(jnp.int32, (TM, 1), 0)
    up = jnp.where(row == 0, prev_row, pltpu.roll(uc, 1, axis=0))
    dn = jnp.where(row == TM - 1, next_row, pltpu.roll(uc, TM - 1, axis=0))
    y = up * cw_ref[0:1, :] + uc * cw_ref[1:2, :] + dn * cw_ref[2:3, :]
    conv = _dot((cb_ref[...].astype(F32) * y).astype(BF16), cwo_ref[...])
    z = _dot(_gelu_tanh(ys_ref[...]).astype(BF16), wglu_ref[...])
    s5o = z[:, :D_MODEL] * _sigmoid(z[:, D_MODEL:])
    g = gate_ref[...].astype(F32)
    merged = g[:, :D_MODEL] * att + g[:, D_MODEL:2 * D_MODEL] * conv + g[:, 2 * D_MODEL:] * s5o
    x = jnp.where(i == 0, ctx_ref[...], lat_ref[...])
    x = x + moda_ref[0] * _dot(merged.astype(BF16), wout_ref[...])
    mod = modm_ref[0]
    h = (_rms(x, gm_ref[...]) * (1.0 + mod[:, D_MODEL:2 * D_MODEL]) + mod[:, :D_MODEL]).astype(BF16)
    acc = jnp.zeros((TM, D_MODEL), F32)
    for c in range(D_FF // D_MODEL):
        a = jnp.maximum(_dot(h, w1_ref[:, c * D_MODEL:(c + 1) * D_MODEL]), 0.0)
        acc = acc + _dot((a * a).astype(BF16), w2_ref[c * D_MODEL:(c + 1) * D_MODEL, :])
    y = x + mod[:, 2 * D_MODEL:] * acc
    if final:
        y = _rms(y, gf_ref[...])
    out_ref[...] = y


def _merge_mlp_call(src, mods, o_src, cb, uc, ys, gate, lw, norm_final, nb, tps, first_tile, final):
    t = cb.shape[0]
    n_tiles = tps - first_tile
    blk = lambda b, i: b * tps + i + first_tile
    rowb = lambda n: pl.BlockSpec((TM, n), lambda b, i: (blk(b, i), 0))
    mod_row = lambda b, i: jnp.where(i + first_tile == 0, nb, b)
    per_halo = TM // HALO
    last_halo = t // HALO - 1
    return pl.pallas_call(
        functools.partial(_merge_mlp_kernel, first_tile=first_tile, tps=tps, final=final),
        grid=(nb, n_tiles),
        in_specs=_token_specs(src, first_tile) + [
                  pl.BlockSpec((1, 1, D_MODEL), lambda b, i: (mod_row(b, i), 0, 2)),
                  pl.BlockSpec((1, 1, 3 * D_MODEL), lambda b, i: (mod_row(b, i), 0, 1))]
                 + _token_specs(o_src, first_tile, N_HEADS * V_DIM) + [
                  rowb(CONV_WIDTH), rowb(CONV_WIDTH),
                  pl.BlockSpec((HALO, CONV_WIDTH), lambda b, i: (jnp.maximum(blk(b, i) * per_halo - 1, 0), 0)),
                  pl.BlockSpec((HALO, CONV_WIDTH),
                               lambda b, i: (jnp.minimum((blk(b, i) + 1) * per_halo, last_halo), 0)),
                  rowb(S5_WIDTH), rowb(N_BRANCH * D_MODEL),
                  _const_spec((N_HEADS * V_DIM, D_MODEL)), _const_spec((CONV_K, CONV_WIDTH)),
                  _const_spec((CONV_WIDTH, D_MODEL)),
                  _const_spec((S5_WIDTH, 2 * D_MODEL)), _const_spec((D_MODEL, D_MODEL)),
                  _const_spec((1, D_MODEL)), _const_spec((D_MODEL, D_FF)), _const_spec((D_FF, D_MODEL)),
                  _const_spec((1, D_MODEL))],
        out_specs=pl.BlockSpec((TM, D_MODEL), lambda b, i: (b * n_tiles + i, 0)),
        out_shape=jax.ShapeDtypeStruct((nb * n_tiles * TM, D_MODEL), F32),
        compiler_params=_cparams(2),
        name="merge_mlp",
    )(src.ctx, src.lat, mods, mods, o_src.ctx, o_src.lat, cb, uc, uc, uc, ys, gate,
      lw["w_o"], lw["conv_w"], lw["conv_w_out"], lw["w_glu"], lw["w_out"], lw["norm_mlp"], lw["w1"], lw["w2"],
      norm_final)


def _rot_half(w):
    wr = w.reshape(w.shape[:-1] + (2, 2, QK_ROPE // 4))
    return jnp.concatenate([-wr[..., 1:, :], wr[..., :1, :]], axis=-2).reshape(w.shape)


def _layer_weights(i, w_in, norm_mix, q_norm, w_uq, kv_norm, w_ukv, w_o, conv_w, conv_w_out, s5_d, w_glu,
                   w_out, norm_mlp, w1, w2):
    wi = w_in[i]
    pe = wi[:, OFF_PE:OFF_CB]
    w_in_p = jnp.concatenate(
        [wi[:, OFF_Q:OFF_PE], wi[:, OFF_CB:OFF_G], wi[:, OFF_G:], pe, _rot_half(pe),
         jnp.zeros((D_MODEL, LANES - 2 * QK_ROPE), F32)], axis=1).astype(BF16)
    uq = w_uq[i].reshape(Q_LORA, N_HEADS, QK_NOPE + QK_ROPE)
    zpad = jnp.zeros((Q_LORA, N_HEADS, HEAD_PAD - QK_NOPE - QK_ROPE), F32)
    wq1 = jnp.concatenate([uq, zpad], axis=-1).reshape(Q_LORA, N_HEADS * HEAD_PAD)
    wq2 = jnp.concatenate([jnp.zeros((Q_LORA, N_HEADS, QK_NOPE), F32), _rot_half(uq[..., QK_NOPE:]), zpad],
                          axis=-1).reshape(Q_LORA, N_HEADS * HEAD_PAD)
    ukv = w_ukv[i].reshape(KV_LORA, N_HEADS, QK_NOPE + V_DIM)
    wk = jnp.concatenate([ukv[..., :QK_NOPE], jnp.zeros((KV_LORA, N_HEADS, HEAD_PAD - QK_NOPE), F32)],
                         axis=-1).reshape(KV_LORA, N_HEADS * HEAD_PAD)
    wv = jnp.concatenate([ukv[..., QK_NOPE:], jnp.zeros((KV_LORA, N_HEADS, HEAD_PAD - V_DIM), F32)],
                         axis=-1).reshape(KV_LORA, N_HEADS * HEAD_PAD)
    return dict(
        w_in=w_in_p, norm_mix=norm_mix[i].reshape(1, D_MODEL),
        q_norm=q_norm[i].reshape(1, Q_LORA), kv_norm=kv_norm[i].reshape(1, KV_LORA),
        wq=jnp.concatenate([wq1, wq2], axis=1).astype(BF16),
        wkt=wk.T.astype(BF16), wv=wv.astype(BF16),
        w_o=w_o[i].astype(BF16), conv_w=conv_w[i], conv_w_out=conv_w_out[i].astype(BF16),
        s5_d=s5_d[i].reshape(1, S5_WIDTH), w_glu=w_glu[i].astype(BF16), w_out=w_out[i].astype(BF16),
        norm_mlp=norm_mlp[i].reshape(1, D_MODEL), w1=w1[i].astype(BF16), w2=w2[i].astype(BF16))


def _rope_tables(n_ctx, n_tokens):
    rows = n_tokens // GRID_W
    pos = jnp.stack([jnp.repeat(jnp.arange(rows), GRID_W), jnp.tile(jnp.arange(GRID_W), rows)], -1).astype(F32)
    n_freq = QK_ROPE // 4
    inv = ROPE_THETA ** (-jnp.arange(n_freq, dtype=F32) / n_freq)
    ang = pos[:, :, None, None] * inv[None, None, None, :]
    ang = jnp.broadcast_to(ang, (n_tokens, 2, 2, n_freq)).reshape(n_tokens, QK_ROPE)
    cos = jnp.concatenate([jnp.ones((n_ctx, QK_ROPE), F32), jnp.cos(ang)], axis=0)
    sin = jnp.concatenate([jnp.zeros((n_ctx, QK_ROPE), F32), jnp.sin(ang)], axis=0)
    s = n_ctx + n_tokens
    one = jnp.ones((s, QK_NOPE), F32)
    z = lambda n: jnp.zeros((s, n), F32)
    pad = HEAD_PAD - QK_NOPE - QK_ROPE
    return jnp.concatenate([one, cos, z(pad), z(QK_NOPE), sin, z(pad), cos, sin, z(LANES - 2 * QK_ROPE)], axis=1)


def _pe_spread():
    j = jnp.arange(QK_ROPE)
    e = jnp.zeros((LANES, N_HEADS * HEAD_PAD), F32)
    for h in range(N_HEADS):
        e = e.at[j, h * HEAD_PAD + QK_NOPE + j].set(1.0)
        e = e.at[QK_ROPE + j, h * HEAD_PAD + QK_NOPE + j].set(1.0)
    return e.astype(BF16)


def kernel(x, c, ctx, c_ctx, ada_w, ada_b, norm_mix, w_in, mla_q_norm, mla_w_uq, mla_kv_norm, mla_w_ukv, mla_w_o, conv_w, conv_w_out, s5_a_re, s5_a_im, s5_log_dt, s5_b_re, s5_b_im, s5_c_re, s5_c_im, s5_d, s5_w_glu, w_out, norm_mlp, mlp_w1, mlp_w2, norm_final):
    nb, n_lat, _ = x.shape
    n_ctx = ctx.shape[1]
    depth = ada_w.shape[0]
    assert n_ctx == TM and n_lat % TQ == 0 and n_lat % GRID_W == 0 and nb == SUBLANES
    s = n_ctx + n_lat
    tps = s // TM
    t = nb * s
    n_chunks = s // S5_CHUNK
    n_ctx_chunks = n_ctx // S5_CHUNK

    c16 = jnp.zeros((16, D_MODEL), F32).at[:nb].set(c).at[nb].set(c_ctx)
    mods = _ada_call(c16, ada_w, ada_b)[:, :nb + 1].reshape(depth, nb + 1, 1, N_MOD * D_MODEL)
    ops = _s5_operators(s5_a_re, s5_a_im, s5_log_dt, s5_b_re, s5_b_im, s5_c_re, s5_c_im)
    rope = _rope_tables(n_ctx, n_lat)
    e_mat = _pe_spread()
    src = _TokenSource(ctx.reshape(nb * n_ctx, D_MODEL), x.reshape(nb * n_lat, D_MODEL), 1, tps - 1, 0)
    gf = norm_final.reshape(1, D_MODEL)

    for i in range(depth):
        last = i == depth - 1
        ft = 1 if last else 0
        lw = _layer_weights(i, w_in, norm_mix, mla_q_norm, mla_w_uq, mla_kv_norm, mla_w_ukv, mla_w_o, conv_w,
                            conv_w_out, s5_d, s5_w_glu, w_out, norm_mlp, mlp_w1, mlp_w2)
        q_ctx, q_lat, kt, v, cb, uc, s5u, gate = _inproj_call(src, mods[i], lw, rope, e_mat, nb, tps)
        o_lat = _attn_call(q_lat, kt, v, nb, TQ, s, s, "attention")
        o_ctx = o_lat if last else _attn_call(q_ctx, kt, v, nb, TM, n_ctx, s, "attention_ctx")
        o_src = _TokenSource(o_ctx, o_lat, 1, tps - 1, 0)
        ys = _s5_call(s5u, ops, lw["s5_d"], i, nb, n_chunks, n_ctx_chunks)
        xs = _merge_mlp_call(src, mods[i], o_src, cb, uc, ys, gate, lw, gf, nb, tps, ft, last)
        src = _TokenSource(xs, xs, tps, tps, 1)
    return xs.reshape(nb, n_lat, D_MODEL)
```

```python
import functools
import math
from typing import NamedTuple

import jax
import jax.numpy as jnp
from jax import lax
from jax.experimental import pallas as pl
from jax.experimental.pallas import tpu as pltpu

F32 = jnp.float32
BF16 = jnp.bfloat16

D_MODEL = 1024
GRID_W = 64
N_HEADS = 8
QK_NOPE = 64
QK_ROPE = 32
V_DIM = 64
Q_LORA = 256
KV_LORA = 256
ROPE_THETA = 10000.0
ROPE_HALF = QK_ROPE // 4
ATTN_SCALE = 1.0 / math.sqrt(QK_NOPE + QK_ROPE)
Q_SCALE = ATTN_SCALE * math.log2(math.e)
CONV_WIDTH = 512
CONV_K = 3
S5_WIDTH = 512
S5_GROUP = 16
S5_GROUPS = S5_WIDTH // S5_GROUP
S5_STATE = 64
N_BRANCH = 3
D_FF = 4 * D_MODEL
EPS = 1e-6
N_MOD = 6

OFF_Q = 0
OFF_KV = OFF_Q + Q_LORA
OFF_PE = OFF_KV + KV_LORA
OFF_CB = OFF_PE + QK_ROPE
OFF_CC = OFF_CB + CONV_WIDTH
OFF_CX = OFF_CC + CONV_WIDTH
OFF_S5 = OFF_CX + CONV_WIDTH
OFF_G = OFF_S5 + S5_WIDTH
IN_COLS = OFF_G + N_BRANCH * D_MODEL

LANES = 128
SUBLANES = 8
HEAD_PAD = LANES

P_QKV = 0
P_CB = P_QKV + Q_LORA + KV_LORA
P_CC = P_CB + CONV_WIDTH
P_CX = P_CC + CONV_WIDTH
P_S5 = P_CX + CONV_WIDTH
P_G = P_S5 + S5_WIDTH
P_PE = P_G + N_BRANCH * D_MODEL
P_COLS = P_PE + LANES

TM = 256
TQ = 512
HALO = 16
S5_CHUNK = 16
S5_ROW = S5_CHUNK * S5_GROUP
GROUPS_PER_SLAB = LANES // S5_GROUP
S5_SLABS = S5_WIDTH // LANES
SCAN_UNROLL = 4
VMEM_LIMIT = 56 * 1024 * 1024


class _TokenSource(NamedTuple):
    ctx: jax.Array
    lat: jax.Array
    ctx_stride: int
    lat_stride: int
    lat_off: int


def _cparams(n_axes):
    return pltpu.CompilerParams(dimension_semantics=("arbitrary",) * n_axes,
                                vmem_limit_bytes=VMEM_LIMIT)


def _const_spec(shape):
    nd = len(shape)
    return pl.BlockSpec(shape, lambda *_: (0,) * nd, pipeline_mode=pl.Buffered(1))


def _rms(x, g):
    return x * lax.rsqrt(jnp.mean(x * x, axis=-1, keepdims=True) + EPS) * g


def _sigmoid(x):
    return 1.0 / (1.0 + jnp.exp(-x))


def _dot(a, b):
    return jnp.dot(a, b, preferred_element_type=F32)


def _ada_kernel(c_ref, w_ref, b_ref, o_ref):
    c = c_ref[...]
    s = (c * _sigmoid(c)).astype(BF16)
    o_ref[0] = _dot(s, w_ref[0].astype(BF16)) + b_ref[0]


def _ada_call(c16, ada_w, ada_b):
    depth = ada_w.shape[0]
    n_col = N_MOD * D_MODEL
    bn = n_col // 4
    return pl.pallas_call(
        _ada_kernel,
        grid=(depth, n_col // bn),
        in_specs=[pl.BlockSpec((16, D_MODEL), lambda l, j: (0, 0)),
                  pl.BlockSpec((1, D_MODEL, bn), lambda l, j: (l, 0, j)),
                  pl.BlockSpec((1, 1, bn), lambda l, j: (l, 0, j))],
        out_specs=pl.BlockSpec((1, 16, bn), lambda l, j: (l, 0, j)),
        out_shape=jax.ShapeDtypeStruct((depth, 16, n_col), F32),
        compiler_params=_cparams(2),
        name="ada_mod",
    )(c16, ada_w, ada_b.reshape(depth, 1, n_col))


def _shift_blocks(x, m):
    lo, hi = x[:, :LANES], x[:, LANES:]
    lane = lax.broadcasted_iota(jnp.int32, lo.shape, 1)
    zero = jnp.zeros_like(lo)
    if m == 0:
        return x
    k = abs(m) % GROUPS_PER_SLAB
    if m > 0:
        cut = S5_GROUP * k
        if m < GROUPS_PER_SLAB:
            r_lo, r_hi = pltpu.roll(lo, cut, axis=1), pltpu.roll(hi, cut, axis=1)
            out = [jnp.where(lane >= cut, r_lo, zero), jnp.where(lane >= cut, r_hi, r_lo)]
        elif k == 0:
            out = [zero, lo]
        else:
            out = [zero, jnp.where(lane >= cut, pltpu.roll(lo, cut, axis=1), zero)]
    else:
        cut = LANES - S5_GROUP * k
        if -m < GROUPS_PER_SLAB:
            r_lo, r_hi = pltpu.roll(lo, cut, axis=1), pltpu.roll(hi, cut, axis=1)
            out = [jnp.where(lane < cut, r_lo, r_hi), jnp.where(lane < cut, r_hi, zero)]
        elif k == 0:
            out = [hi, zero]
        else:
            out = [jnp.where(lane < cut, pltpu.roll(hi, cut, axis=1), zero), zero]
    return jnp.concatenate(out, axis=1)


def _s5prep_kernel(are_ref, aim_ref, ldt_ref, bre_ref, bim_ref, cre_ref, cim_ref,
                   w1_ref, cmre_ref, cmim_ref, a16re_ref, a16im_ref):
    ck = S5_CHUNK
    rows = lax.broadcasted_iota(jnp.int32, (S5_ROW, 1), 0) // S5_GROUP
    fwd = lax.broadcasted_iota(jnp.int32, (1, 2 * S5_STATE), 1) < S5_STATE
    dn = (((1,), (1,)), ((), ()))
    hi = lax.Precision.HIGHEST
    for g in range(GROUPS_PER_SLAB):
        are = are_ref[0, g:g + 1, :]
        aim = aim_ref[0, g:g + 1, :]
        dt = jnp.exp(ldt_ref[0, g:g + 1, :])

        def power(p):
            m = jnp.exp(p * (dt * are))
            ang = p * (dt * aim)
            return m * jnp.cos(ang), m * jnp.sin(ang)

        abre, abim = power(1.0)
        den = are * are + aim * aim
        nr = abre - 1.0
        fre = (nr * are + abim * aim) / den
        fim = (abim * are - nr * aim) / den
        b_re, b_im = bre_ref[0, g], bim_ref[0, g]
        bb_re = fre * b_re - fim * b_im
        bb_im = fre * b_im + fim * b_re
        tile = lambda t: jnp.concatenate([t] * ck, axis=0)
        step = SUBLANES * (rows // SUBLANES) + ((rows % SUBLANES - g) & (SUBLANES - 1))
        stepf = step.astype(F32)
        p_re, p_im = power(jnp.where(fwd, float(ck - 1) - stepf, stepf))
        t_re, t_im = tile(bb_re), tile(bb_im)
        w1_ref[0, g, :, S5_ROW:S5_ROW + 2 * S5_STATE] = (p_re * t_re - p_im * t_im).astype(BF16)
        w1_ref[0, g, :, S5_ROW + 2 * S5_STATE:] = (p_re * t_im + p_im * t_re).astype(BF16)
        c_re, c_im = tile(cre_ref[0, g]), tile(cim_ref[0, g])
        p_re, p_im = power(jnp.where(fwd, stepf + 1.0, float(ck) - stepf))
        cmre_ref[0, g] = (c_re * p_re - c_im * p_im).T.astype(BF16)
        cmim_ref[0, g] = (-(c_re * p_im + c_im * p_re)).T.astype(BF16)
        lag = rows.astype(F32)
        p_re, p_im = power(jnp.where(fwd, lag, float(ck - 1) - lag))
        x_re = p_re * c_re - p_im * c_im
        x_im = p_re * c_im + p_im * c_re

        def kernels(mask):
            return (lax.dot_general(jnp.where(mask, bb_re, 0.0), x_re, dn, precision=hi, preferred_element_type=F32)
                    - lax.dot_general(jnp.where(mask, bb_im, 0.0), x_im, dn, precision=hi,
                                      preferred_element_type=F32))

        kf = kernels(fwd)
        kb = kernels(jnp.logical_not(fwd))
        for pos in range(ck):
            s = SUBLANES * (pos // SUBLANES) + (pos % SUBLANES - g) % SUBLANES
            blockrow = _shift_blocks(kf, s) + _shift_blocks(kb, s - (ck - 1))
            halves = [blockrow[:, :LANES], blockrow[:, LANES:]]
            if g:
                halves = [pltpu.roll(t, S5_GROUP * g, axis=1) for t in halves]
            w1_ref[0, g, pos * S5_GROUP:(pos + 1) * S5_GROUP, :S5_ROW] = jnp.concatenate(halves, axis=1).astype(BF16)
        e16 = jnp.exp(float(ck) * (dt * are))
        a16re_ref[0, g:g + 1, :] = e16 * jnp.cos(float(ck) * (dt * aim))
        a16im_ref[0, g:g + 1, :] = e16 * jnp.sin(float(ck) * (dt * aim))


def _s5_operators(a_re, a_im, log_dt, b_re, b_im, c_re, c_im):
    depth = a_re.shape[0]
    n = depth * S5_SLABS
    gl = GROUPS_PER_SLAB
    vec = lambda t: jnp.moveaxis(t, 1, 2).reshape(n, gl, 2 * S5_STATE)
    ldt = jnp.broadcast_to(jnp.moveaxis(log_dt, 1, 2)[..., None], (depth, S5_GROUPS, 2, S5_STATE))
    bmat = lambda t: jnp.transpose(t, (0, 2, 4, 1, 3)).reshape(n, gl, S5_GROUP, 2 * S5_STATE)
    cmat = lambda t: jnp.transpose(t, (0, 2, 3, 1, 4)).reshape(n, gl, S5_GROUP, 2 * S5_STATE)
    vspec = pl.BlockSpec((1, gl, 2 * S5_STATE), lambda m: (m, 0, 0))
    mspec = pl.BlockSpec((1, gl, S5_GROUP, 2 * S5_STATE), lambda m: (m, 0, 0, 0))
    w1, cre, cim, a16re, a16im = pl.pallas_call(
        _s5prep_kernel,
        grid=(n,),
        in_specs=[vspec, vspec, vspec, mspec, mspec, mspec, mspec],
        out_specs=[pl.BlockSpec((1, gl, S5_ROW, S5_ROW + 4 * S5_STATE), lambda m: (m, 0, 0, 0)),
                   pl.BlockSpec((1, gl, 2 * S5_STATE, S5_ROW), lambda m: (m, 0, 0, 0)),
                   pl.BlockSpec((1, gl, 2 * S5_STATE, S5_ROW), lambda m: (m, 0, 0, 0)), vspec, vspec],
        out_shape=[jax.ShapeDtypeStruct((n, gl, S5_ROW, S5_ROW + 4 * S5_STATE), BF16),
                   jax.ShapeDtypeStruct((n, gl, 2 * S5_STATE, S5_ROW), BF16),
                   jax.ShapeDtypeStruct((n, gl, 2 * S5_STATE, S5_ROW), BF16),
                   jax.ShapeDtypeStruct((n, gl, 2 * S5_STATE), F32), jax.ShapeDtypeStruct((n, gl, 2 * S5_STATE), F32)],
        compiler_params=_cparams(1),
        name="s5_operators",
    )(vec(a_re), vec(a_im), ldt.reshape(n, gl, 2 * S5_STATE), bmat(b_re), bmat(b_im), cmat(c_re), cmat(c_im))
    r = lambda t: t.reshape((depth, S5_SLABS) + t.shape[1:])
    return dict(w1=r(w1), cre=r(cre), cim=r(cim), are=r(a16re), aim=r(a16im))


def _s5_kernel(z_ref, w1_ref, cre_ref, cim_ref, are_ref, aim_ref, d_ref, y_ref,
               yin, lre, lim, sre_f, sre_b, sim_f, sim_b, *, n_chunks, n_ctx_chunks, pitch):
    ng = GROUPS_PER_SLAB
    blk = lax.broadcasted_iota(jnp.int32, (n_chunks, LANES), 1) // S5_GROUP

    def step_rows(j):
        return z_ref[pl.ds(j, n_chunks, stride=S5_CHUNK), :]

    rot = [[step_rows(SUBLANES * h + s) if s == 0
            else pltpu.roll(step_rows(SUBLANES * h + s), S5_GROUP * s, axis=1)
            for s in range(SUBLANES)] for h in range(2)]
    for g in range(ng):
        halves = []
        for h in range(2):
            u = rot[h][(0 - g) % ng]
            for q in range(1, ng):
                u = jnp.where(blk == q, rot[h][(q - g) % ng], u)
            halves.append(u)
        r = _dot(jnp.concatenate(halves, axis=1).astype(BF16), w1_ref[g])
        yin[g] = r[:, :S5_ROW]
        lre[pl.ds(g * pitch, n_chunks), :] = r[:, S5_ROW:S5_ROW + 2 * S5_STATE]
        lim[pl.ds(g * pitch, n_chunks), :] = r[:, S5_ROW + 2 * S5_STATE:]

    ar = are_ref[...]
    ai = aim_ref[...]
    fwd_lane = lax.broadcasted_iota(jnp.int32, (ng, 2 * S5_STATE), 1) < S5_STATE

    def body(kstep, carry):
        st_re, st_im = carry
        rb = jnp.where(kstep < n_ctx_chunks, n_ctx_chunks - 1 - kstep, n_chunks - 1 + n_ctx_chunks - kstep)
        rows_f = pl.ds(kstep, ng, stride=pitch)
        rows_b = pl.ds(rb, ng, stride=pitch)
        sre_f[rows_f, :] = st_re
        sre_b[rows_b, :] = st_re
        sim_f[rows_f, :] = st_im
        sim_b[rows_b, :] = st_im
        loc_re = jnp.where(fwd_lane, lre[rows_f, :], lre[rows_b, :])
        loc_im = jnp.where(fwd_lane, lim[rows_f, :], lim[rows_b, :])
        return ar * st_re - ai * st_im + loc_re, ar * st_im + ai * st_re + loc_im

    zero = jnp.zeros((ng, 2 * S5_STATE), F32)
    lax.fori_loop(0, n_chunks, body, (zero, zero), unroll=SCAN_UNROLL)

    fwd_big = lax.broadcasted_iota(jnp.int32, (n_chunks, 2 * S5_STATE), 1) < S5_STATE
    for g in range(ng):
        rows = pl.ds(g * pitch, n_chunks)
        s_re = jnp.where(fwd_big, sre_f[rows, :], sre_b[rows, :]).astype(BF16)
        s_im = jnp.where(fwd_big, sim_f[rows, :], sim_b[rows, :]).astype(BF16)
        yin[g] = yin[g] + _dot(s_re, cre_ref[g]) + _dot(s_im, cim_ref[g])

    d = d_ref[...]
    for h in range(2):
        for s in range(SUBLANES):
            v = yin[(0 - s) % ng, :, h * LANES:(h + 1) * LANES]
            for q in range(1, ng):
                v = jnp.where(blk == q, yin[(q - s) % ng, :, h * LANES:(h + 1) * LANES], v)
            if s:
                v = pltpu.roll(v, LANES - S5_GROUP * s, axis=1)
            j = SUBLANES * h + s
            y_ref[pl.ds(j, n_chunks, stride=S5_CHUNK), :] = v + d * step_rows(j)


def _s5_call(s5u, ops, s5_d, layer, nb, n_chunks, n_ctx_chunks):
    t = s5u.shape[0]
    s = t // nb
    pitch = -(-n_chunks // SUBLANES) * SUBLANES
    if (pitch // SUBLANES) % 2 == 0:
        pitch += SUBLANES
    gl = GROUPS_PER_SLAB
    mat = lambda r, c: pl.BlockSpec((None, None, gl, r, c), lambda a, b: (layer, a, 0, 0, 0))
    vec = pl.BlockSpec((None, None, gl, 2 * S5_STATE), lambda a, b: (layer, a, 0, 0))
    state = pltpu.VMEM((gl * pitch, 2 * S5_STATE), F32)
    return pl.pallas_call(
        functools.partial(_s5_kernel, n_chunks=n_chunks, n_ctx_chunks=n_ctx_chunks, pitch=pitch),
        grid=(S5_SLABS, nb),
        in_specs=[pl.BlockSpec((s, LANES), lambda a, b: (b, a)),
                  mat(S5_ROW, S5_ROW + 4 * S5_STATE), mat(2 * S5_STATE, S5_ROW), mat(2 * S5_STATE, S5_ROW),
                  vec, vec, pl.BlockSpec((1, LANES), lambda a, b: (0, a))],
        out_specs=pl.BlockSpec((s, LANES), lambda a, b: (b, a)),
        out_shape=jax.ShapeDtypeStruct((t, S5_WIDTH), F32),
        scratch_shapes=[pltpu.VMEM((gl, n_chunks, S5_ROW), F32)] + [state] * 6,
        compiler_params=_cparams(2),
        name="s5_scan",
    )(s5u, ops["w1"], ops["cre"], ops["cim"], ops["are"], ops["aim"], s5_d)


def _rot_half_lanes(x, first_half):
    return jnp.where(first_half, -pltpu.roll(x, LANES - ROPE_HALF, axis=1), pltpu.roll(x, ROPE_HALF, axis=1))


def _inproj_kernel(ctx_ref, lat_ref, mod_ref, g_ref, w_ref, qn_ref, kvn_ref, wq_ref, wkv_ref, rope_ref,
                   qc_out, ql_out, k_out, v_out, cb_out, uc_out, s5_out, gate_out):
    mod = mod_ref[0]
    x = jnp.where(pl.program_id(1) == 0, ctx_ref[...], lat_ref[...])
    xn = _rms(x, g_ref[...]) * (1.0 + mod[:, D_MODEL:]) + mod[:, :D_MODEL]
    xn = xn.astype(BF16)
    nq = N_HEADS * HEAD_PAD

    def proj(a, n):
        return _dot(xn, w_ref[:, a:a + n])

    def gate(c):
        gate_out[:, c * D_MODEL:(c + 1) * D_MODEL] = _sigmoid(proj(P_G + c * D_MODEL, D_MODEL)).astype(BF16)

    zqkv = proj(P_QKV, Q_LORA + KV_LORA)
    gate(0)
    qn = _rms(zqkv[:, :Q_LORA], qn_ref[...]).astype(BF16)
    kvn = _rms(zqkv[:, Q_LORA:], kvn_ref[...]).astype(BF16)
    q1 = _dot(qn, wq_ref[...])
    kv = _dot(kvn, wkv_ref[...])
    zpe = proj(P_PE, LANES)
    gate(1)
    rope = rope_ref[...]
    cos_q, sin_q = rope[:, 0:LANES], rope[:, LANES:2 * LANES]
    cos_k, sin_k = rope[:, 2 * LANES:3 * LANES], rope[:, 3 * LANES:4 * LANES]
    first_half = lax.broadcasted_iota(jnp.int32, (1, LANES), 1) % (2 * ROPE_HALF) < ROPE_HALF
    kpe = zpe * cos_k + _rot_half_lanes(zpe, first_half) * sin_k
    kslot = pltpu.roll(kpe, QK_NOPE, axis=1)
    q_heads, k_heads = [], []
    for h in range(N_HEADS):
        sl = slice(h * HEAD_PAD, (h + 1) * HEAD_PAD)
        qh = q1[:, sl]
        q_heads.append(((qh * cos_q + _rot_half_lanes(qh, first_half) * sin_q) * Q_SCALE).astype(BF16))
        k_heads.append((kv[:, sl] + kslot).astype(BF16))
    q = jnp.concatenate(q_heads, axis=1)
    k_out[...] = jnp.concatenate(k_heads, axis=1)

    @pl.when(pl.program_id(1) == 0)
    def _():
        qc_out[...] = q

    @pl.when(pl.program_id(1) != 0)
    def _():
        ql_out[...] = q

    one_lane = lax.broadcasted_iota(jnp.int32, (1, nq), 1) % HEAD_PAD == V_DIM
    v_out[...] = (kv[:, nq:] + jnp.where(one_lane, 1.0, 0.0)).astype(BF16)
    gate(2)
    cb_out[...] = proj(P_CB, CONV_WIDTH).astype(BF16)
    uc_out[...] = (proj(P_CC, CONV_WIDTH) * proj(P_CX, CONV_WIDTH)).astype(BF16)
    s5_out[...] = proj(P_S5, S5_WIDTH)


def _token_specs(src, first_tile, width=D_MODEL):
    return [pl.BlockSpec((TM, width), lambda b, i: (b * src.ctx_stride, 0)),
            pl.BlockSpec((TM, width),
                         lambda b, i: (b * src.lat_stride + src.lat_off + jnp.maximum(i + first_tile - 1, 0), 0))]


def _inproj_call(src, mods, lw, rope, nb, tps):
    t = nb * tps * TM
    rowb = lambda n: pl.BlockSpec((TM, n), lambda b, i: (b * tps + i, 0))
    nq = N_HEADS * HEAD_PAD
    outs = [(nq, BF16), (nq, BF16), (CONV_WIDTH, BF16), (CONV_WIDTH, BF16), (S5_WIDTH, F32),
            (N_BRANCH * D_MODEL, BF16)]
    q_specs = [pl.BlockSpec((TM, nq), lambda b, i: (b, 0)),
               pl.BlockSpec((TM, nq), lambda b, i: (b * (tps - 1) + jnp.maximum(i - 1, 0), 0))]
    q_shapes = [jax.ShapeDtypeStruct((nb * TM, nq), BF16), jax.ShapeDtypeStruct((nb * (tps - 1) * TM, nq), BF16)]
    return pl.pallas_call(
        _inproj_kernel,
        grid=(nb, tps),
        in_specs=_token_specs(src, 0) + [
                  pl.BlockSpec((1, 1, 2 * D_MODEL), lambda b, i: (jnp.where(i == 0, nb, b), 0, 0)),
                  _const_spec((1, D_MODEL)),
                  _const_spec((D_MODEL, P_COLS)),
                  _const_spec((1, Q_LORA)), _const_spec((1, KV_LORA)),
                  _const_spec((Q_LORA, nq)), _const_spec((KV_LORA, 2 * nq)),
                  pl.BlockSpec((TM, 4 * LANES), lambda b, i: (i, 0))],
        out_specs=q_specs + [rowb(n) for n, _ in outs],
        out_shape=q_shapes + [jax.ShapeDtypeStruct((t, n), dt) for n, dt in outs],
        compiler_params=_cparams(2),
        name="in_projection",
    )(src.ctx, src.lat, mods, lw["norm_mix"], lw["w_in"], lw["q_norm"], lw["kv_norm"], lw["wq"], lw["wkv"],
      rope)


def _attn_head(q, k, v):
    s = lax.dot_general(q, k, (((1,), (1,)), ((), ())), preferred_element_type=F32)
    p = jnp.exp2(s - jnp.max(s, axis=-1, keepdims=True)).astype(BF16)
    oe = _dot(p, v)
    return oe[:, :V_DIM] / oe[:, V_DIM:V_DIM + 1]


def _attn_kernel(q_ref, k_ref, v_ref, o_ref):
    outs = []
    for h in range(N_HEADS):
        sl = slice(h * HEAD_PAD, (h + 1) * HEAD_PAD)
        outs.append(_attn_head(q_ref[:, sl], k_ref[:, sl], v_ref[:, sl]))
    o_ref[...] = jnp.concatenate(outs, axis=-1).astype(BF16)


def _attn_call(q, k, v, nb, rows_q, rows_kv, kv_stride, name):
    nq = N_HEADS * HEAD_PAD
    nv = N_HEADS * V_DIM
    n_q = q.shape[0] // (nb * rows_q)
    kv_map = lambda b, i: (b * (kv_stride // rows_kv), 0)
    k_spec = pl.BlockSpec((rows_kv, nq), kv_map)
    v_spec = pl.BlockSpec((rows_kv, nq), kv_map, pipeline_mode=pl.Buffered(1))
    return pl.pallas_call(
        _attn_kernel,
        grid=(nb, n_q),
        in_specs=[pl.BlockSpec((rows_q, nq), lambda b, i: (b * n_q + i, 0)), k_spec, v_spec],
        out_specs=pl.BlockSpec((rows_q, nv), lambda b, i: (b * n_q + i, 0)),
        out_shape=jax.ShapeDtypeStruct((q.shape[0], nv), BF16),
        compiler_params=_cparams(2),
        name=name,
    )(q, k, v)


def _gelu_tanh(x):
    return 0.5 * x * (1.0 + jnp.tanh(math.sqrt(2.0 / math.pi) * (x + 0.044715 * (x * x * x))))


def _merge_mlp_kernel(ctx_ref, lat_ref, moda_ref, modm_ref, oc_ref, ol_ref, cb_ref, uc_ref, ucp_ref, ucn_ref, ys_ref,
                      gate_ref, wo_ref, cw_ref, cwo_ref, wglu_ref, wout_ref, gm_ref, w1_ref, w2_ref, gf_ref, out_ref,
                      *, first_tile, tps, final):
    i = pl.program_id(1) + first_tile
    att = _dot(jnp.where(i == 0, oc_ref[...], ol_ref[...]), wo_ref[...])
    uc = uc_ref[...].astype(F32)
    prev_row = jnp.where(i >= 2, ucp_ref[HALO - 1:HALO, :].astype(F32), 0.0)
    next_row = jnp.where(jnp.logical_and(i >= 1, i < tps - 1), ucn_ref[0:1, :].astype(F32), 0.0)
    row = lax.broadcasted_iota(jnp.int32, (TM, 1), 0)
    up = jnp.where(row == 0, prev_row, pltpu.roll(uc, 1, axis=0))
    dn = jnp.where(row == TM - 1, next_row, pltpu.roll(uc, TM - 1, axis=0))
    y = up * cw_ref[0:1, :] + uc * cw_ref[1:2, :] + dn * cw_ref[2:3, :]
    conv = _dot((cb_ref[...].astype(F32) * y).astype(BF16), cwo_ref[...])
    z = _dot(_gelu_tanh(ys_ref[...]).astype(BF16), wglu_ref[...])
    s5o = z[:, :D_MODEL] * _sigmoid(z[:, D_MODEL:])
    g = gate_ref[...].astype(F32)
    merged = g[:, :D_MODEL] * att + g[:, D_MODEL:2 * D_MODEL] * conv + g[:, 2 * D_MODEL:] * s5o
    x = jnp.where(i == 0, ctx_ref[...], lat_ref[...])
    x = x + moda_ref[0] * _dot(merged.astype(BF16), wout_ref[...])
    mod = modm_ref[0]
    h = (_rms(x, gm_ref[...]) * (1.0 + mod[:, D_MODEL:2 * D_MODEL]) + mod[:, :D_MODEL]).astype(BF16)
    acc = jnp.zeros((TM, D_MODEL), F32)
    for c in range(D_FF // D_MODEL):
        a = jnp.maximum(_dot(h, w1_ref[:, c * D_MODEL:(c + 1) * D_MODEL]), 0.0)
        acc = acc + _dot((a * a).astype(BF16), w2_ref[c * D_MODEL:(c + 1) * D_MODEL, :])
    y = x + mod[:, 2 * D_MODEL:] * acc
    if final:
        y = _rms(y, gf_ref[...])
    out_ref[...] = y


def _merge_mlp_call(src, mods, o_src, cb, uc, ys, gate, lw, norm_final, nb, tps, first_tile, final):
    t = cb.shape[0]
    n_tiles = tps - first_tile
    blk = lambda b, i: b * tps + i + first_tile
    rowb = lambda n: pl.BlockSpec((TM, n), lambda b, i: (blk(b, i), 0))
    mod_row = lambda b, i: jnp.where(i + first_tile == 0, nb, b)
    per_halo = TM // HALO
    last_halo = t // HALO - 1
    return pl.pallas_call(
        functools.partial(_merge_mlp_kernel, first_tile=first_tile, tps=tps, final=final),
        grid=(nb, n_tiles),
        in_specs=_token_specs(src, first_tile) + [
                  pl.BlockSpec((1, 1, D_MODEL), lambda b, i: (mod_row(b, i), 0, 2)),
                  pl.BlockSpec((1, 1, 3 * D_MODEL), lambda b, i: (mod_row(b, i), 0, 1))]
                 + _token_specs(o_src, first_tile, N_HEADS * V_DIM) + [
                  rowb(CONV_WIDTH), rowb(CONV_WIDTH),
                  pl.BlockSpec((HALO, CONV_WIDTH), lambda b, i: (jnp.maximum(blk(b, i) * per_halo - 1, 0), 0)),
                  pl.BlockSpec((HALO, CONV_WIDTH),
                               lambda b, i: (jnp.minimum((blk(b, i) + 1) * per_halo, last_halo), 0)),
                  rowb(S5_WIDTH), rowb(N_BRANCH * D_MODEL),
                  _const_spec((N_HEADS * V_DIM, D_MODEL)), _const_spec((CONV_K, CONV_WIDTH)),
                  _const_spec((CONV_WIDTH, D_MODEL)),
                  _const_spec((S5_WIDTH, 2 * D_MODEL)), _const_spec((D_MODEL, D_MODEL)),
                  _const_spec((1, D_MODEL)), _const_spec((D_MODEL, D_FF)), _const_spec((D_FF, D_MODEL)),
                  _const_spec((1, D_MODEL))],
        out_specs=pl.BlockSpec((TM, D_MODEL), lambda b, i: (b * n_tiles + i, 0)),
        out_shape=jax.ShapeDtypeStruct((nb * n_tiles * TM, D_MODEL), F32),
        compiler_params=_cparams(2),
        name="merge_mlp",
    )(src.ctx, src.lat, mods, mods, o_src.ctx, o_src.lat, cb, uc, uc, uc, ys, gate,
      lw["w_o"], lw["conv_w"], lw["conv_w_out"], lw["w_glu"], lw["w_out"], lw["norm_mlp"], lw["w1"], lw["w2"],
      norm_final)


def _layer_weights(i, w_in, norm_mix, q_norm, w_uq, kv_norm, w_ukv, w_o, conv_w, conv_w_out, s5_d, w_glu,
                   w_out, norm_mlp, w1, w2):
    wi = w_in[i]
    pe = wi[:, OFF_PE:OFF_CB]
    w_in_p = jnp.concatenate(
        [wi[:, OFF_Q:OFF_PE], wi[:, OFF_CB:OFF_G], wi[:, OFF_G:], pe,
         jnp.zeros((D_MODEL, LANES - QK_ROPE), F32)], axis=1).astype(BF16)
    uq = w_uq[i].reshape(Q_LORA, N_HEADS, QK_NOPE + QK_ROPE)
    zpad = jnp.zeros((Q_LORA, N_HEADS, HEAD_PAD - QK_NOPE - QK_ROPE), F32)
    wq = jnp.concatenate([uq, zpad], axis=-1).reshape(Q_LORA, N_HEADS * HEAD_PAD)
    ukv = w_ukv[i].reshape(KV_LORA, N_HEADS, QK_NOPE + V_DIM)
    wk = jnp.concatenate([ukv[..., :QK_NOPE], jnp.zeros((KV_LORA, N_HEADS, HEAD_PAD - QK_NOPE), F32)],
                         axis=-1).reshape(KV_LORA, N_HEADS * HEAD_PAD)
    wv = jnp.concatenate([ukv[..., QK_NOPE:], jnp.zeros((KV_LORA, N_HEADS, HEAD_PAD - V_DIM), F32)],
                         axis=-1).reshape(KV_LORA, N_HEADS * HEAD_PAD)
    return dict(
        w_in=w_in_p, norm_mix=norm_mix[i].reshape(1, D_MODEL),
        q_norm=q_norm[i].reshape(1, Q_LORA), kv_norm=kv_norm[i].reshape(1, KV_LORA),
        wq=wq.astype(BF16),
        wkv=jnp.concatenate([wk, wv], axis=1).astype(BF16),
        w_o=w_o[i].astype(BF16), conv_w=conv_w[i], conv_w_out=conv_w_out[i].astype(BF16),
        s5_d=s5_d[i].reshape(1, S5_WIDTH), w_glu=w_glu[i].astype(BF16), w_out=w_out[i].astype(BF16),
        norm_mlp=norm_mlp[i].reshape(1, D_MODEL), w1=w1[i].astype(BF16), w2=w2[i].astype(BF16))


def _rope_tables(n_ctx, n_tokens):
    rows = n_tokens // GRID_W
    pos = jnp.stack([jnp.repeat(jnp.arange(rows), GRID_W), jnp.tile(jnp.arange(GRID_W), rows)], -1).astype(F32)
    n_freq = QK_ROPE // 4
    inv = ROPE_THETA ** (-jnp.arange(n_freq, dtype=F32) / n_freq)
    ang = pos[:, :, None, None] * inv[None, None, None, :]
    ang = jnp.broadcast_to(ang, (n_tokens, 2, 2, n_freq)).reshape(n_tokens, QK_ROPE)
    cos = jnp.concatenate([jnp.ones((n_ctx, QK_ROPE), F32), jnp.cos(ang)], axis=0)
    sin = jnp.concatenate([jnp.zeros((n_ctx, QK_ROPE), F32), jnp.sin(ang)], axis=0)
    s = n_ctx + n_tokens
    one = jnp.ones((s, QK_NOPE), F32)
    z = lambda n: jnp.zeros((s, n), F32)
    pad = HEAD_PAD - QK_NOPE - QK_ROPE
    return jnp.concatenate([one, cos, z(pad), z(QK_NOPE), sin, z(pad),
                            cos, z(LANES - QK_ROPE), sin, z(LANES - QK_ROPE)], axis=1)


def kernel(x, c, ctx, c_ctx, ada_w, ada_b, norm_mix, w_in, mla_q_norm, mla_w_uq, mla_kv_norm, mla_w_ukv, mla_w_o, conv_w, conv_w_out, s5_a_re, s5_a_im, s5_log_dt, s5_b_re, s5_b_im, s5_c_re, s5_c_im, s5_d, s5_w_glu, w_out, norm_mlp, mlp_w1, mlp_w2, norm_final):
    nb, n_lat, _ = x.shape
    n_ctx = ctx.shape[1]
    depth = ada_w.shape[0]
    assert n_ctx == TM and n_lat % TQ == 0 and n_lat % GRID_W == 0 and nb == SUBLANES
    s = n_ctx + n_lat
    tps = s // TM
    t = nb * s
    n_chunks = s // S5_CHUNK
    n_ctx_chunks = n_ctx // S5_CHUNK

    c16 = jnp.zeros((16, D_MODEL), F32).at[:nb].set(c).at[nb].set(c_ctx)
    mods = _ada_call(c16, ada_w, ada_b)[:, :nb + 1].reshape(depth, nb + 1, 1, N_MOD * D_MODEL)
    ops = _s5_operators(s5_a_re, s5_a_im, s5_log_dt, s5_b_re, s5_b_im, s5_c_re, s5_c_im)
    rope = _rope_tables(n_ctx, n_lat)
    src = _TokenSource(ctx.reshape(nb * n_ctx, D_MODEL), x.reshape(nb * n_lat, D_MODEL), 1, tps - 1, 0)
    gf = norm_final.reshape(1, D_MODEL)

    for i in range(depth):
        last = i == depth - 1
        ft = 1 if last else 0
        lw = _layer_weights(i, w_in, norm_mix, mla_q_norm, mla_w_uq, mla_kv_norm, mla_w_ukv, mla_w_o, conv_w,
                            conv_w_out, s5_d, s5_w_glu, w_out, norm_mlp, mlp_w1, mlp_w2)
        q_ctx, q_lat, k, v, cb, uc, s5u, gate = _inproj_call(src, mods[i], lw, rope, nb, tps)
        o_lat = _attn_call(q_lat, k, v, nb, TQ, s, s, "attention")
        o_ctx = o_lat if last else _attn_call(q_ctx, k, v, nb, TM, n_ctx, s, "attention_ctx")
        o_src = _TokenSource(o_ctx, o_lat, 1, tps - 1, 0)
        ys = _s5_call(s5u, ops, lw["s5_d"], i, nb, n_chunks, n_ctx_chunks)
        xs = _merge_mlp_call(src, mods[i], o_src, cb, uc, ys, gate, lw, gf, nb, tps, ft, last)
        src = _TokenSource(xs, xs, tps, tps, 1)
    return xs.reshape(nb, n_lat, D_MODEL)
```

```python
import functools
import math
from typing import NamedTuple

import jax
import jax.numpy as jnp
from jax import lax
from jax.experimental import pallas as pl
from jax.experimental.pallas import tpu as pltpu

F32 = jnp.float32
BF16 = jnp.bfloat16

D_MODEL = 1024
GRID_W = 64
N_HEADS = 8
QK_NOPE = 64
QK_ROPE = 32
V_DIM = 64
Q_LORA = 256
KV_LORA = 256
ROPE_THETA = 10000.0
ROPE_HALF = QK_ROPE // 4
ATTN_SCALE = 1.0 / math.sqrt(QK_NOPE + QK_ROPE)
Q_SCALE = ATTN_SCALE * math.log2(math.e)
CONV_WIDTH = 512
CONV_K = 3
S5_WIDTH = 512
S5_GROUP = 16
S5_GROUPS = S5_WIDTH // S5_GROUP
S5_STATE = 64
N_BRANCH = 3
D_FF = 4 * D_MODEL
EPS = 1e-6
N_MOD = 6

OFF_Q = 0
OFF_KV = OFF_Q + Q_LORA
OFF_PE = OFF_KV + KV_LORA
OFF_CB = OFF_PE + QK_ROPE
OFF_CC = OFF_CB + CONV_WIDTH
OFF_CX = OFF_CC + CONV_WIDTH
OFF_S5 = OFF_CX + CONV_WIDTH
OFF_G = OFF_S5 + S5_WIDTH
IN_COLS = OFF_G + N_BRANCH * D_MODEL

LANES = 128
SUBLANES = 8
HEAD_PAD = LANES

P_QKV = 0
P_CB = P_QKV + Q_LORA + KV_LORA
P_CC = P_CB + CONV_WIDTH
P_CX = P_CC + CONV_WIDTH
P_S5 = P_CX + CONV_WIDTH
P_G = P_S5 + S5_WIDTH
P_PE = P_G + N_BRANCH * D_MODEL
P_COLS = P_PE + LANES

TM = 256
TQ = 512
HALO = 16
S5_CHUNK = 16
S5_ROW = S5_CHUNK * S5_GROUP
GROUPS_PER_SLAB = LANES // S5_GROUP
S5_SLABS = S5_WIDTH // LANES
SCAN_UNROLL = 4
POWER_ROWS = 24
VMEM_LIMIT = 56 * 1024 * 1024


class _TokenSource(NamedTuple):
    ctx: jax.Array
    lat: jax.Array
    ctx_stride: int
    lat_stride: int
    lat_off: int


def _cparams(n_axes):
    return pltpu.CompilerParams(dimension_semantics=("arbitrary",) * n_axes,
                                vmem_limit_bytes=VMEM_LIMIT)


def _const_spec(shape):
    nd = len(shape)
    return pl.BlockSpec(shape, lambda *_: (0,) * nd, pipeline_mode=pl.Buffered(1))


def _rms(x, g):
    return x * lax.rsqrt(jnp.mean(x * x, axis=-1, keepdims=True) + EPS) * g


def _sigmoid(x):
    return 1.0 / (1.0 + jnp.exp(-x))


def _dot(a, b):
    return jnp.dot(a, b, preferred_element_type=F32)


def _ada_kernel(c_ref, w_ref, b_ref, o_ref):
    c = c_ref[...]
    s = (c * _sigmoid(c)).astype(BF16)
    o_ref[0] = _dot(s, w_ref[0].astype(BF16)) + b_ref[0]


def _ada_call(c16, ada_w, ada_b):
    depth = ada_w.shape[0]
    n_col = N_MOD * D_MODEL
    bn = n_col // 4
    return pl.pallas_call(
        _ada_kernel,
        grid=(depth, n_col // bn),
        in_specs=[pl.BlockSpec((16, D_MODEL), lambda l, j: (0, 0)),
                  pl.BlockSpec((1, D_MODEL, bn), lambda l, j: (l, 0, j)),
                  pl.BlockSpec((1, 1, bn), lambda l, j: (l, 0, j))],
        out_specs=pl.BlockSpec((1, 16, bn), lambda l, j: (l, 0, j)),
        out_shape=jax.ShapeDtypeStruct((depth, 16, n_col), F32),
        compiler_params=_cparams(2),
        name="ada_mod",
    )(c16, ada_w, ada_b.reshape(depth, 1, n_col))


def _shift_blocks(x, m):
    lo, hi = x[:, :LANES], x[:, LANES:]
    lane = lax.broadcasted_iota(jnp.int32, lo.shape, 1)
    zero = jnp.zeros_like(lo)
    if m == 0:
        return x
    k = abs(m) % GROUPS_PER_SLAB
    if m > 0:
        cut = S5_GROUP * k
        if m < GROUPS_PER_SLAB:
            r_lo, r_hi = pltpu.roll(lo, cut, axis=1), pltpu.roll(hi, cut, axis=1)
            out = [jnp.where(lane >= cut, r_lo, zero), jnp.where(lane >= cut, r_hi, r_lo)]
        elif k == 0:
            out = [zero, lo]
        else:
            out = [zero, jnp.where(lane >= cut, pltpu.roll(lo, cut, axis=1), zero)]
    else:
        cut = LANES - S5_GROUP * k
        if -m < GROUPS_PER_SLAB:
            r_lo, r_hi = pltpu.roll(lo, cut, axis=1), pltpu.roll(hi, cut, axis=1)
            out = [jnp.where(lane < cut, r_lo, r_hi), jnp.where(lane < cut, r_hi, zero)]
        elif k == 0:
            out = [hi, zero]
        else:
            out = [jnp.where(lane < cut, pltpu.roll(hi, cut, axis=1), zero), zero]
    return jnp.concatenate(out, axis=1)


def _s5prep_kernel(are_ref, aim_ref, ldt_ref, bre_ref, bim_ref, cre_ref, cim_ref,
                   w1_ref, cmre_ref, cmim_ref, a16re_ref, a16im_ref):
    ck = S5_CHUNK
    rows = lax.broadcasted_iota(jnp.int32, (S5_ROW, 1), 0) // S5_GROUP
    fwd = lax.broadcasted_iota(jnp.int32, (1, 2 * S5_STATE), 1) < S5_STATE
    dn = (((1,), (1,)), ((), ()))
    hi = lax.Precision.HIGHEST
    for g in range(GROUPS_PER_SLAB):
        are = are_ref[0, g:g + 1, :]
        aim = aim_ref[0, g:g + 1, :]
        dt = jnp.exp(ldt_ref[0, g:g + 1, :])

        p_col = lax.broadcasted_iota(jnp.int32, (POWER_ROWS, 1), 0).astype(F32)
        mag = jnp.exp(p_col * (dt * are))
        tab_re = mag * jnp.cos(p_col * (dt * aim))
        tab_im = mag * jnp.sin(p_col * (dt * aim))

        def power(pidx):
            acc_re = jnp.broadcast_to(tab_re[0:1, :], pidx.shape)
            acc_im = jnp.broadcast_to(tab_im[0:1, :], pidx.shape)
            for p in range(1, ck + 1):
                hit = pidx == p
                acc_re = jnp.where(hit, tab_re[p:p + 1, :], acc_re)
                acc_im = jnp.where(hit, tab_im[p:p + 1, :], acc_im)
            return acc_re, acc_im

        abre, abim = tab_re[1:2, :], tab_im[1:2, :]
        den = are * are + aim * aim
        nr = abre - 1.0
        fre = (nr * are + abim * aim) / den
        fim = (abim * are - nr * aim) / den
        b_re, b_im = bre_ref[0, g], bim_ref[0, g]
        bb_re = fre * b_re - fim * b_im
        bb_im = fre * b_im + fim * b_re
        tile = lambda t: jnp.concatenate([t] * ck, axis=0)
        step = SUBLANES * (rows // SUBLANES) + ((rows % SUBLANES - g) & (SUBLANES - 1))
        p_re, p_im = power(jnp.where(fwd, ck - 1 - step, step))
        t_re, t_im = tile(bb_re), tile(bb_im)
        w1_ref[0, g, :, S5_ROW:S5_ROW + 2 * S5_STATE] = (p_re * t_re - p_im * t_im).astype(BF16)
        w1_ref[0, g, :, S5_ROW + 2 * S5_STATE:] = (p_re * t_im + p_im * t_re).astype(BF16)
        c_re, c_im = tile(cre_ref[0, g]), tile(cim_ref[0, g])
        p_re, p_im = power(jnp.where(fwd, step + 1, ck - step))
        cmre_ref[0, g] = (c_re * p_re - c_im * p_im).T.astype(BF16)
        cmim_ref[0, g] = (-(c_re * p_im + c_im * p_re)).T.astype(BF16)
        p_re, p_im = power(jnp.where(fwd, rows, ck - 1 - rows))
        x_re = p_re * c_re - p_im * c_im
        x_im = p_re * c_im + p_im * c_re

        def kernels(mask):
            return (lax.dot_general(jnp.where(mask, bb_re, 0.0), x_re, dn, precision=hi, preferred_element_type=F32)
                    - lax.dot_general(jnp.where(mask, bb_im, 0.0), x_im, dn, precision=hi,
                                      preferred_element_type=F32))

        kf = kernels(fwd)
        kb = kernels(jnp.logical_not(fwd))
        for pos in range(ck):
            s = SUBLANES * (pos // SUBLANES) + (pos % SUBLANES - g) % SUBLANES
            blockrow = _shift_blocks(kf, s) + _shift_blocks(kb, s - (ck - 1))
            halves = [blockrow[:, :LANES], blockrow[:, LANES:]]
            if g:
                halves = [pltpu.roll(t, S5_GROUP * g, axis=1) for t in halves]
            w1_ref[0, g, pos * S5_GROUP:(pos + 1) * S5_GROUP, :S5_ROW] = jnp.concatenate(halves, axis=1).astype(BF16)
        a16re_ref[0, g:g + 1, :] = tab_re[ck:ck + 1, :]
        a16im_ref[0, g:g + 1, :] = tab_im[ck:ck + 1, :]


def _s5_operators(a_re, a_im, log_dt, b_re, b_im, c_re, c_im):
    depth = a_re.shape[0]
    n = depth * S5_SLABS
    gl = GROUPS_PER_SLAB
    vec = lambda t: jnp.moveaxis(t, 1, 2).reshape(n, gl, 2 * S5_STATE)
    ldt = jnp.broadcast_to(jnp.moveaxis(log_dt, 1, 2)[..., None], (depth, S5_GROUPS, 2, S5_STATE))
    bmat = lambda t: jnp.transpose(t, (0, 2, 4, 1, 3)).reshape(n, gl, S5_GROUP, 2 * S5_STATE)
    cmat = lambda t: jnp.transpose(t, (0, 2, 3, 1, 4)).reshape(n, gl, S5_GROUP, 2 * S5_STATE)
    vspec = pl.BlockSpec((1, gl, 2 * S5_STATE), lambda m: (m, 0, 0))
    mspec = pl.BlockSpec((1, gl, S5_GROUP, 2 * S5_STATE), lambda m: (m, 0, 0, 0))
    w1, cre, cim, a16re, a16im = pl.pallas_call(
        _s5prep_kernel,
        grid=(n,),
        in_specs=[vspec, vspec, vspec, mspec, mspec, mspec, mspec],
        out_specs=[pl.BlockSpec((1, gl, S5_ROW, S5_ROW + 4 * S5_STATE), lambda m: (m, 0, 0, 0)),
                   pl.BlockSpec((1, gl, 2 * S5_STATE, S5_ROW), lambda m: (m, 0, 0, 0)),
                   pl.BlockSpec((1, gl, 2 * S5_STATE, S5_ROW), lambda m: (m, 0, 0, 0)), vspec, vspec],
        out_shape=[jax.ShapeDtypeStruct((n, gl, S5_ROW, S5_ROW + 4 * S5_STATE), BF16),
                   jax.ShapeDtypeStruct((n, gl, 2 * S5_STATE, S5_ROW), BF16),
                   jax.ShapeDtypeStruct((n, gl, 2 * S5_STATE, S5_ROW), BF16),
                   jax.ShapeDtypeStruct((n, gl, 2 * S5_STATE), F32), jax.ShapeDtypeStruct((n, gl, 2 * S5_STATE), F32)],
        compiler_params=_cparams(1),
        name="s5_operators",
    )(vec(a_re), vec(a_im), ldt.reshape(n, gl, 2 * S5_STATE), bmat(b_re), bmat(b_im), cmat(c_re), cmat(c_im))
    r = lambda t: t.reshape((depth, S5_SLABS) + t.shape[1:])
    return dict(w1=r(w1), cre=r(cre), cim=r(cim), are=r(a16re), aim=r(a16im))


def _s5_kernel(z_ref, w1_ref, cre_ref, cim_ref, are_ref, aim_ref, d_ref, y_ref,
               yin, lre, lim, sre_f, sre_b, sim_f, sim_b, *, n_chunks, n_ctx_chunks, pitch):
    ng = GROUPS_PER_SLAB
    blk = lax.broadcasted_iota(jnp.int32, (n_chunks, LANES), 1) // S5_GROUP

    def step_rows(j):
        return z_ref[pl.ds(j, n_chunks, stride=S5_CHUNK), :]

    rot = [[step_rows(SUBLANES * h + s) if s == 0
            else pltpu.roll(step_rows(SUBLANES * h + s), S5_GROUP * s, axis=1)
            for s in range(SUBLANES)] for h in range(2)]
    for g in range(ng):
        halves = []
        for h in range(2):
            u = rot[h][(0 - g) % ng]
            for q in range(1, ng):
                u = jnp.where(blk == q, rot[h][(q - g) % ng], u)
            halves.append(u)
        r = _dot(jnp.concatenate(halves, axis=1).astype(BF16), w1_ref[g])
        yin[g] = r[:, :S5_ROW]
        lre[pl.ds(g * pitch, n_chunks), :] = r[:, S5_ROW:S5_ROW + 2 * S5_STATE]
        lim[pl.ds(g * pitch, n_chunks), :] = r[:, S5_ROW + 2 * S5_STATE:]

    ar = are_ref[...]
    ai = aim_ref[...]
    fwd_lane = lax.broadcasted_iota(jnp.int32, (ng, 2 * S5_STATE), 1) < S5_STATE

    def body(kstep, carry):
        st_re, st_im = carry
        rb = jnp.where(kstep < n_ctx_chunks, n_ctx_chunks - 1 - kstep, n_chunks - 1 + n_ctx_chunks - kstep)
        rows_f = pl.ds(kstep, ng, stride=pitch)
        rows_b = pl.ds(rb, ng, stride=pitch)
        sre_f[rows_f, :] = st_re
        sre_b[rows_b, :] = st_re
        sim_f[rows_f, :] = st_im
        sim_b[rows_b, :] = st_im
        loc_re = jnp.where(fwd_lane, lre[rows_f, :], lre[rows_b, :])
        loc_im = jnp.where(fwd_lane, lim[rows_f, :], lim[rows_b, :])
        return ar * st_re - ai * st_im + loc_re, ar * st_im + ai * st_re + loc_im

    zero = jnp.zeros((ng, 2 * S5_STATE), F32)
    lax.fori_loop(0, n_chunks, body, (zero, zero), unroll=SCAN_UNROLL)

    fwd_big = lax.broadcasted_iota(jnp.int32, (n_chunks, 2 * S5_STATE), 1) < S5_STATE
    for g in range(ng):
        rows = pl.ds(g * pitch, n_chunks)
        s_re = jnp.where(fwd_big, sre_f[rows, :], sre_b[rows, :]).astype(BF16)
        s_im = jnp.where(fwd_big, sim_f[rows, :], sim_b[rows, :]).astype(BF16)
        yin[g] = yin[g] + _dot(s_re, cre_ref[g]) + _dot(s_im, cim_ref[g])

    d = d_ref[...]
    for h in range(2):
        for s in range(SUBLANES):
            v = yin[(0 - s) % ng, :, h * LANES:(h + 1) * LANES]
            for q in range(1, ng):
                v = jnp.where(blk == q, yin[(q - s) % ng, :, h * LANES:(h + 1) * LANES], v)
            if s:
                v = pltpu.roll(v, LANES - S5_GROUP * s, axis=1)
            j = SUBLANES * h + s
            y_ref[pl.ds(j, n_chunks, stride=S5_CHUNK), :] = v + d * step_rows(j)


def _s5_call(s5u, ops, s5_d, layer, nb, n_chunks, n_ctx_chunks):
    t = s5u.shape[0]
    s = t // nb
    pitch = -(-n_chunks // SUBLANES) * SUBLANES
    if (pitch // SUBLANES) % 2 == 0:
        pitch += SUBLANES
    gl = GROUPS_PER_SLAB
    mat = lambda r, c: pl.BlockSpec((None, None, gl, r, c), lambda a, b: (layer, a, 0, 0, 0))
    vec = pl.BlockSpec((None, None, gl, 2 * S5_STATE), lambda a, b: (layer, a, 0, 0))
    state = pltpu.VMEM((gl * pitch, 2 * S5_STATE), F32)
    return pl.pallas_call(
        functools.partial(_s5_kernel, n_chunks=n_chunks, n_ctx_chunks=n_ctx_chunks, pitch=pitch),
        grid=(S5_SLABS, nb),
        in_specs=[pl.BlockSpec((s, LANES), lambda a, b: (b, a)),
                  mat(S5_ROW, S5_ROW + 4 * S5_STATE), mat(2 * S5_STATE, S5_ROW), mat(2 * S5_STATE, S5_ROW),
                  vec, vec, pl.BlockSpec((1, LANES), lambda a, b: (0, a))],
        out_specs=pl.BlockSpec((s, LANES), lambda a, b: (b, a)),
        out_shape=jax.ShapeDtypeStruct((t, S5_WIDTH), F32),
        scratch_shapes=[pltpu.VMEM((gl, n_chunks, S5_ROW), F32)] + [state] * 6,
        compiler_params=_cparams(2),
        name="s5_scan",
    )(s5u, ops["w1"], ops["cre"], ops["cim"], ops["are"], ops["aim"], s5_d)


def _rot_half_lanes(x, first_half):
    return jnp.where(first_half, -pltpu.roll(x, LANES - ROPE_HALF, axis=1), pltpu.roll(x, ROPE_HALF, axis=1))


def _inproj_kernel(ctx_ref, lat_ref, mod_ref, g_ref, w_ref, qn_ref, kvn_ref, wq_ref, wkv_ref, rope_ref,
                   qc_out, ql_out, k_out, v_out, cb_out, uc_out, s5_out, gate_out):
    mod = mod_ref[0]
    x = jnp.where(pl.program_id(1) == 0, ctx_ref[...], lat_ref[...])
    xn = _rms(x, g_ref[...]) * (1.0 + mod[:, D_MODEL:]) + mod[:, :D_MODEL]
    xn = xn.astype(BF16)
    nq = N_HEADS * HEAD_PAD

    def proj(a, n):
        return _dot(xn, w_ref[:, a:a + n])

    def gate(c):
        gate_out[:, c * D_MODEL:(c + 1) * D_MODEL] = _sigmoid(proj(P_G + c * D_MODEL, D_MODEL)).astype(BF16)

    zqkv = proj(P_QKV, Q_LORA + KV_LORA)
    gate(0)
    qn = _rms(zqkv[:, :Q_LORA], qn_ref[...]).astype(BF16)
    kvn = _rms(zqkv[:, Q_LORA:], kvn_ref[...]).astype(BF16)
    q1 = _dot(qn, wq_ref[...])
    kv = _dot(kvn, wkv_ref[...])
    zpe = proj(P_PE, LANES)
    gate(1)
    rope = rope_ref[...]
    cos_q, sin_q = rope[:, 0:LANES], rope[:, LANES:2 * LANES]
    cos_k, sin_k = rope[:, 2 * LANES:3 * LANES], rope[:, 3 * LANES:4 * LANES]
    first_half = lax.broadcasted_iota(jnp.int32, (1, LANES), 1) % (2 * ROPE_HALF) < ROPE_HALF
    kpe = zpe * cos_k + _rot_half_lanes(zpe, first_half) * sin_k
    kslot = pltpu.roll(kpe, QK_NOPE, axis=1)
    q_heads, k_heads = [], []
    for h in range(N_HEADS):
        sl = slice(h * HEAD_PAD, (h + 1) * HEAD_PAD)
        qh = q1[:, sl]
        q_heads.append(((qh * cos_q + _rot_half_lanes(qh, first_half) * sin_q) * Q_SCALE).astype(BF16))
        k_heads.append((kv[:, sl] + kslot).astype(BF16))
    q = jnp.concatenate(q_heads, axis=1)
    k_out[...] = jnp.concatenate(k_heads, axis=1)
    qc_out[...] = q
    ql_out[...] = q
    one_lane = lax.broadcasted_iota(jnp.int32, (1, nq), 1) % HEAD_PAD == V_DIM
    v_out[...] = (kv[:, nq:] + jnp.where(one_lane, 1.0, 0.0)).astype(BF16)
    gate(2)
    cb_out[...] = proj(P_CB, CONV_WIDTH).astype(BF16)
    uc_out[...] = (proj(P_CC, CONV_WIDTH) * proj(P_CX, CONV_WIDTH)).astype(BF16)
    s5_out[...] = proj(P_S5, S5_WIDTH)


def _token_specs(src, first_tile, width=D_MODEL):
    return [pl.BlockSpec((TM, width), lambda b, i: (b * src.ctx_stride, 0)),
            pl.BlockSpec((TM, width),
                         lambda b, i: (b * src.lat_stride + src.lat_off + jnp.maximum(i + first_tile - 1, 0), 0))]


def _inproj_call(src, mods, lw, rope, nb, tps):
    t = nb * tps * TM
    rowb = lambda n: pl.BlockSpec((TM, n), lambda b, i: (b * tps + i, 0))
    nq = N_HEADS * HEAD_PAD
    outs = [(nq, BF16), (nq, BF16), (CONV_WIDTH, BF16), (CONV_WIDTH, BF16), (S5_WIDTH, F32),
            (N_BRANCH * D_MODEL, BF16)]
    q_specs = [pl.BlockSpec((TM, nq), lambda b, i: (jnp.where(i == 0, b, nb + b), 0)),
               pl.BlockSpec((TM, nq), lambda b, i: (b * (tps - 1) + jnp.maximum(i - 1, 0), 0))]
    q_shapes = [jax.ShapeDtypeStruct((2 * nb * TM, nq), BF16),
                jax.ShapeDtypeStruct((nb * (tps - 1) * TM, nq), BF16)]
    return pl.pallas_call(
        _inproj_kernel,
        grid=(nb, tps),
        in_specs=_token_specs(src, 0) + [
                  pl.BlockSpec((1, 1, 2 * D_MODEL), lambda b, i: (jnp.where(i == 0, nb, b), 0, 0)),
                  _const_spec((1, D_MODEL)),
                  _const_spec((D_MODEL, P_COLS)),
                  _const_spec((1, Q_LORA)), _const_spec((1, KV_LORA)),
                  _const_spec((Q_LORA, nq)), _const_spec((KV_LORA, 2 * nq)),
                  pl.BlockSpec((TM, 4 * LANES), lambda b, i: (i, 0))],
        out_specs=q_specs + [rowb(n) for n, _ in outs],
        out_shape=q_shapes + [jax.ShapeDtypeStruct((t, n), dt) for n, dt in outs],
        compiler_params=_cparams(2),
        name="in_projection",
    )(src.ctx, src.lat, mods, lw["norm_mix"], lw["w_in"], lw["q_norm"], lw["kv_norm"], lw["wq"], lw["wkv"],
      rope)


def _attn_head(q, k, v):
    s = lax.dot_general(q, k, (((1,), (1,)), ((), ())), preferred_element_type=F32)
    p = jnp.exp2(s - jnp.max(s, axis=-1, keepdims=True)).astype(BF16)
    oe = _dot(p, v)
    return oe[:, :V_DIM] / oe[:, V_DIM:V_DIM + 1]


def _attn_kernel(q_ref, k_ref, v_ref, o_ref):
    outs = []
    for h in range(N_HEADS):
        sl = slice(h * HEAD_PAD, (h + 1) * HEAD_PAD)
        outs.append(_attn_head(q_ref[:, sl], k_ref[:, sl], v_ref[:, sl]))
    o_ref[...] = jnp.concatenate(outs, axis=-1).astype(BF16)


def _attn_call(q, k, v, nb, n_q, rows_q, rows_kv, kv_stride, name):
    nq = N_HEADS * HEAD_PAD
    nv = N_HEADS * V_DIM
    kv_map = lambda b, i: (b * (kv_stride // rows_kv), 0)
    k_spec = pl.BlockSpec((rows_kv, nq), kv_map)
    v_spec = pl.BlockSpec((rows_kv, nq), kv_map, pipeline_mode=pl.Buffered(1))
    return pl.pallas_call(
        _attn_kernel,
        grid=(nb, n_q),
        in_specs=[pl.BlockSpec((rows_q, nq), lambda b, i: (b * n_q + i, 0)), k_spec, v_spec],
        out_specs=pl.BlockSpec((rows_q, nv), lambda b, i: (b * n_q + i, 0)),
        out_shape=jax.ShapeDtypeStruct((nb * n_q * rows_q, nv), BF16),
        compiler_params=_cparams(2),
        name=name,
    )(q, k, v)


def _gelu_tanh(x):
    return 0.5 * x * (1.0 + jnp.tanh(math.sqrt(2.0 / math.pi) * (x + 0.044715 * (x * x * x))))


def _merge_mlp_kernel(ctx_ref, lat_ref, moda_ref, modm_ref, oc_ref, ol_ref, cb_ref, uc_ref, ucp_ref, ucn_ref, ys_ref,
                      gate_ref, wo_ref, cw_ref, cwo_ref, wglu_ref, wout_ref, gm_ref, w1_ref, w2_ref, gf_ref, out_ref,
                      *, first_tile, tps, final):
    i = pl.program_id(1) + first_tile
    att = _dot(jnp.where(i == 0, oc_ref[...], ol_ref[...]), wo_ref[...])
    uc = uc_ref[...].astype(F32)
    prev_row = jnp.where(i >= 2, ucp_ref[HALO - 1:HALO, :].astype(F32), 0.0)
    next_row = jnp.where(jnp.logical_and(i >= 1, i < tps - 1), ucn_ref[0:1, :].astype(F32), 0.0)
    row = lax.broadcasted_iota(jnp.int32, (TM, 1), 0)
    up = jnp.where(row == 0, prev_row, pltpu.roll(uc, 1, axis=0))
    dn = jnp.where(row == TM - 1, next_row, pltpu.roll(uc, TM - 1, axis=0))
    y = up * cw_ref[0:1, :] + uc * cw_ref[1:2, :] + dn * cw_ref[2:3, :]
    conv = _dot((cb_ref[...].astype(F32) * y).astype(BF16), cwo_ref[...])
    z = _dot(_gelu_tanh(ys_ref[...]).astype(BF16), wglu_ref[...])
    s5o = z[:, :D_MODEL] * _sigmoid(z[:, D_MODEL:])
    g = gate_ref[...].astype(F32)
    merged = g[:, :D_MODEL] * att + g[:, D_MODEL:2 * D_MODEL] * conv + g[:, 2 * D_MODEL:] * s5o
    x = jnp.where(i == 0, ctx_ref[...], lat_ref[...])
    x = x + moda_ref[0] * _dot(merged.astype(BF16), wout_ref[...])
    mod = modm_ref[0]
    h = (_rms(x, gm_ref[...]) * (1.0 + mod[:, D_MODEL:2 * D_MODEL]) + mod[:, :D_MODEL]).astype(BF16)
    acc = jnp.zeros((TM, D_MODEL), F32)
    for c in range(D_FF // D_MODEL):
        a = jnp.maximum(_dot(h, w1_ref[:, c * D_MODEL:(c + 1) * D_MODEL]), 0.0)
        acc = acc + _dot((a * a).astype(BF16), w2_ref[c * D_MODEL:(c + 1) * D_MODEL, :])
    y = x + mod[:, 2 * D_MODEL:] * acc
    if final:
        y = _rms(y, gf_ref[...])
    out_ref[...] = y


def _merge_mlp_call(src, mods, o_src, cb, uc, ys, gate, lw, norm_final, nb, tps, first_tile, final):
    t = cb.shape[0]
    n_tiles = tps - first_tile
    blk = lambda b, i: b * tps + i + first_tile
    rowb = lambda n: pl.BlockSpec((TM, n), lambda b, i: (blk(b, i), 0))
    mod_row = lambda b, i: jnp.where(i + first_tile == 0, nb, b)
    per_halo = TM // HALO
    last_halo = t // HALO - 1
    return pl.pallas_call(
        functools.partial(_merge_mlp_kernel, first_tile=first_tile, tps=tps, final=final),
        grid=(nb, n_tiles),
        in_specs=_token_specs(src, first_tile) + [
                  pl.BlockSpec((1, 1, D_MODEL), lambda b, i: (mod_row(b, i), 0, 2)),
                  pl.BlockSpec((1, 1, 3 * D_MODEL), lambda b, i: (mod_row(b, i), 0, 1))]
                 + _token_specs(o_src, first_tile, N_HEADS * V_DIM) + [
                  rowb(CONV_WIDTH), rowb(CONV_WIDTH),
                  pl.BlockSpec((HALO, CONV_WIDTH), lambda b, i: (jnp.maximum(blk(b, i) * per_halo - 1, 0), 0)),
                  pl.BlockSpec((HALO, CONV_WIDTH),
                               lambda b, i: (jnp.minimum((blk(b, i) + 1) * per_halo, last_halo), 0)),
                  rowb(S5_WIDTH), rowb(N_BRANCH * D_MODEL),
                  _const_spec((N_HEADS * V_DIM, D_MODEL)), _const_spec((CONV_K, CONV_WIDTH)),
                  _const_spec((CONV_WIDTH, D_MODEL)),
                  _const_spec((S5_WIDTH, 2 * D_MODEL)), _const_spec((D_MODEL, D_MODEL)),
                  _const_spec((1, D_MODEL)), _const_spec((D_MODEL, D_FF)), _const_spec((D_FF, D_MODEL)),
                  _const_spec((1, D_MODEL))],
        out_specs=pl.BlockSpec((TM, D_MODEL), lambda b, i: (b * n_tiles + i, 0)),
        out_shape=jax.ShapeDtypeStruct((nb * n_tiles * TM, D_MODEL), F32),
        compiler_params=_cparams(2),
        name="merge_mlp",
    )(src.ctx, src.lat, mods, mods, o_src.ctx, o_src.lat, cb, uc, uc, uc, ys, gate,
      lw["w_o"], lw["conv_w"], lw["conv_w_out"], lw["w_glu"], lw["w_out"], lw["norm_mlp"], lw["w1"], lw["w2"],
      norm_final)


def _layer_weights(i, w_in, norm_mix, q_norm, w_uq, kv_norm, w_ukv, w_o, conv_w, conv_w_out, s5_d, w_glu,
                   w_out, norm_mlp, w1, w2):
    wi = w_in[i]
    pe = wi[:, OFF_PE:OFF_CB]
    w_in_p = jnp.concatenate(
        [wi[:, OFF_Q:OFF_PE], wi[:, OFF_CB:OFF_G], wi[:, OFF_G:], pe,
         jnp.zeros((D_MODEL, LANES - QK_ROPE), F32)], axis=1).astype(BF16)
    uq = w_uq[i].reshape(Q_LORA, N_HEADS, QK_NOPE + QK_ROPE)
    zpad = jnp.zeros((Q_LORA, N_HEADS, HEAD_PAD - QK_NOPE - QK_ROPE), F32)
    wq = jnp.concatenate([uq, zpad], axis=-1).reshape(Q_LORA, N_HEADS * HEAD_PAD)
    ukv = w_ukv[i].reshape(KV_LORA, N_HEADS, QK_NOPE + V_DIM)
    wk = jnp.concatenate([ukv[..., :QK_NOPE], jnp.zeros((KV_LORA, N_HEADS, HEAD_PAD - QK_NOPE), F32)],
                         axis=-1).reshape(KV_LORA, N_HEADS * HEAD_PAD)
    wv = jnp.concatenate([ukv[..., QK_NOPE:], jnp.zeros((KV_LORA, N_HEADS, HEAD_PAD - V_DIM), F32)],
                         axis=-1).reshape(KV_LORA, N_HEADS * HEAD_PAD)
    return dict(
        w_in=w_in_p, norm_mix=norm_mix[i].reshape(1, D_MODEL),
        q_norm=q_norm[i].reshape(1, Q_LORA), kv_norm=kv_norm[i].reshape(1, KV_LORA),
        wq=wq.astype(BF16),
        wkv=jnp.concatenate([wk, wv], axis=1).astype(BF16),
        w_o=w_o[i].astype(BF16), conv_w=conv_w[i], conv_w_out=conv_w_out[i].astype(BF16),
        s5_d=s5_d[i].reshape(1, S5_WIDTH), w_glu=w_glu[i].astype(BF16), w_out=w_out[i].astype(BF16),
        norm_mlp=norm_mlp[i].reshape(1, D_MODEL), w1=w1[i].astype(BF16), w2=w2[i].astype(BF16))


def _rope_tables(n_ctx, n_tokens):
    rows = n_tokens // GRID_W
    pos = jnp.stack([jnp.repeat(jnp.arange(rows), GRID_W), jnp.tile(jnp.arange(GRID_W), rows)], -1).astype(F32)
    n_freq = QK_ROPE // 4
    inv = ROPE_THETA ** (-jnp.arange(n_freq, dtype=F32) / n_freq)
    ang = pos[:, :, None, None] * inv[None, None, None, :]
    ang = jnp.broadcast_to(ang, (n_tokens, 2, 2, n_freq)).reshape(n_tokens, QK_ROPE)
    cos = jnp.concatenate([jnp.ones((n_ctx, QK_ROPE), F32), jnp.cos(ang)], axis=0)
    sin = jnp.concatenate([jnp.zeros((n_ctx, QK_ROPE), F32), jnp.sin(ang)], axis=0)
    s = n_ctx + n_tokens
    one = jnp.ones((s, QK_NOPE), F32)
    z = lambda n: jnp.zeros((s, n), F32)
    pad = HEAD_PAD - QK_NOPE - QK_ROPE
    return jnp.concatenate([one, cos, z(pad), z(QK_NOPE), sin, z(pad),
                            cos, z(LANES - QK_ROPE), sin, z(LANES - QK_ROPE)], axis=1)


def kernel(x, c, ctx, c_ctx, ada_w, ada_b, norm_mix, w_in, mla_q_norm, mla_w_uq, mla_kv_norm, mla_w_ukv, mla_w_o, conv_w, conv_w_out, s5_a_re, s5_a_im, s5_log_dt, s5_b_re, s5_b_im, s5_c_re, s5_c_im, s5_d, s5_w_glu, w_out, norm_mlp, mlp_w1, mlp_w2, norm_final):
    nb, n_lat, _ = x.shape
    n_ctx = ctx.shape[1]
    depth = ada_w.shape[0]
    assert n_ctx == TM and n_lat % TQ == 0 and n_lat % GRID_W == 0 and nb == SUBLANES
    s = n_ctx + n_lat
    tps = s // TM
    t = nb * s
    n_chunks = s // S5_CHUNK
    n_ctx_chunks = n_ctx // S5_CHUNK

    c16 = jnp.zeros((16, D_MODEL), F32).at[:nb].set(c).at[nb].set(c_ctx)
    mods = _ada_call(c16, ada_w, ada_b)[:, :nb + 1].reshape(depth, nb + 1, 1, N_MOD * D_MODEL)
    ops = _s5_operators(s5_a_re, s5_a_im, s5_log_dt, s5_b_re, s5_b_im, s5_c_re, s5_c_im)
    rope = _rope_tables(n_ctx, n_lat)
    src = _TokenSource(ctx.reshape(nb * n_ctx, D_MODEL), x.reshape(nb * n_lat, D_MODEL), 1, tps - 1, 0)
    gf = norm_final.reshape(1, D_MODEL)

    for i in range(depth):
        last = i == depth - 1
        ft = 1 if last else 0
        lw = _layer_weights(i, w_in, norm_mix, mla_q_norm, mla_w_uq, mla_kv_norm, mla_w_ukv, mla_w_o, conv_w,
                            conv_w_out, s5_d, s5_w_glu, w_out, norm_mlp, mlp_w1, mlp_w2)
        q_ctx, q_lat, k, v, cb, uc, s5u, gate = _inproj_call(src, mods[i], lw, rope, nb, tps)
        o_lat = _attn_call(q_lat, k, v, nb, n_lat // TQ, TQ, s, s, "attention")
        o_ctx = o_lat if last else _attn_call(q_ctx, k, v, nb, 1, TM, n_ctx, s, "attention_ctx")
        o_src = _TokenSource(o_ctx, o_lat, 1, tps - 1, 0)
        ys = _s5_call(s5u, ops, lw["s5_d"], i, nb, n_chunks, n_ctx_chunks)
        xs = _merge_mlp_call(src, mods[i], o_src, cb, uc, ys, gate, lw, gf, nb, tps, ft, last)
        src = _TokenSource(xs, xs, tps, tps, 1)
    return xs.reshape(nb, n_lat, D_MODEL)
```

```python
import functools
import math
from typing import NamedTuple

import jax
import jax.numpy as jnp
from jax import lax
from jax.experimental import pallas as pl
from jax.experimental.pallas import tpu as pltpu

F32 = jnp.float32
BF16 = jnp.bfloat16

D_MODEL = 1024
GRID_W = 64
N_HEADS = 8
QK_NOPE = 64
QK_ROPE = 32
V_DIM = 64
Q_LORA = 256
KV_LORA = 256
ROPE_THETA = 10000.0
ROPE_HALF = QK_ROPE // 4
ATTN_SCALE = 1.0 / math.sqrt(QK_NOPE + QK_ROPE)
Q_SCALE = ATTN_SCALE * math.log2(math.e)
CONV_WIDTH = 512
CONV_K = 3
S5_WIDTH = 512
S5_GROUP = 16
S5_GROUPS = S5_WIDTH // S5_GROUP
S5_STATE = 64
N_BRANCH = 3
D_FF = 4 * D_MODEL
EPS = 1e-6
N_MOD = 6

OFF_Q = 0
OFF_KV = OFF_Q + Q_LORA
OFF_PE = OFF_KV + KV_LORA
OFF_CB = OFF_PE + QK_ROPE
OFF_CC = OFF_CB + CONV_WIDTH
OFF_CX = OFF_CC + CONV_WIDTH
OFF_S5 = OFF_CX + CONV_WIDTH
OFF_G = OFF_S5 + S5_WIDTH
IN_COLS = OFF_G + N_BRANCH * D_MODEL

LANES = 128
SUBLANES = 8
HEAD_PAD = LANES

P_QKV = 0
P_CB = P_QKV + Q_LORA + KV_LORA
P_CC = P_CB + CONV_WIDTH
P_CX = P_CC + CONV_WIDTH
P_S5 = P_CX + CONV_WIDTH
P_G = P_S5 + S5_WIDTH
P_PE = P_G + N_BRANCH * D_MODEL
P_COLS = P_PE + LANES

TM = 256
TQ = 512
HALO = 16
S5_CHUNK = 16
S5_ROW = S5_CHUNK * S5_GROUP
GROUPS_PER_SLAB = LANES // S5_GROUP
S5_SLABS = S5_WIDTH // LANES
SCAN_UNROLL = 4
S5_ASM_ROWS = 16
POWER_ROWS = 24
VMEM_LIMIT = 56 * 1024 * 1024


class _TokenSource(NamedTuple):
    ctx: jax.Array
    lat: jax.Array
    ctx_stride: int
    lat_stride: int
    lat_off: int


def _cparams(n_axes):
    return pltpu.CompilerParams(dimension_semantics=("arbitrary",) * n_axes,
                                vmem_limit_bytes=VMEM_LIMIT)


def _const_spec(shape):
    nd = len(shape)
    return pl.BlockSpec(shape, lambda *_: (0,) * nd, pipeline_mode=pl.Buffered(1))


def _rms(x, g):
    return x * lax.rsqrt(jnp.mean(x * x, axis=-1, keepdims=True) + EPS) * g


def _sigmoid(x):
    return 1.0 / (1.0 + jnp.exp(-x))


def _dot(a, b):
    return jnp.dot(a, b, preferred_element_type=F32)


def _ada_kernel(c_ref, w_ref, b_ref, o_ref):
    c = c_ref[...]
    s = (c * _sigmoid(c)).astype(BF16)
    o_ref[0] = _dot(s, w_ref[0].astype(BF16)) + b_ref[0]


def _ada_call(c16, ada_w, ada_b):
    depth = ada_w.shape[0]
    n_col = N_MOD * D_MODEL
    bn = n_col // 4
    return pl.pallas_call(
        _ada_kernel,
        grid=(depth, n_col // bn),
        in_specs=[pl.BlockSpec((16, D_MODEL), lambda l, j: (0, 0)),
                  pl.BlockSpec((1, D_MODEL, bn), lambda l, j: (l, 0, j)),
                  pl.BlockSpec((1, 1, bn), lambda l, j: (l, 0, j))],
        out_specs=pl.BlockSpec((1, 16, bn), lambda l, j: (l, 0, j)),
        out_shape=jax.ShapeDtypeStruct((depth, 16, n_col), F32),
        compiler_params=_cparams(2),
        name="ada_mod",
    )(c16, ada_w, ada_b.reshape(depth, 1, n_col))


def _shift_blocks(x, m):
    lo, hi = x[:, :LANES], x[:, LANES:]
    lane = lax.broadcasted_iota(jnp.int32, lo.shape, 1)
    zero = jnp.zeros_like(lo)
    if m == 0:
        return x
    k = abs(m) % GROUPS_PER_SLAB
    if m > 0:
        cut = S5_GROUP * k
        if m < GROUPS_PER_SLAB:
            r_lo, r_hi = pltpu.roll(lo, cut, axis=1), pltpu.roll(hi, cut, axis=1)
            out = [jnp.where(lane >= cut, r_lo, zero), jnp.where(lane >= cut, r_hi, r_lo)]
        elif k == 0:
            out = [zero, lo]
        else:
            out = [zero, jnp.where(lane >= cut, pltpu.roll(lo, cut, axis=1), zero)]
    else:
        cut = LANES - S5_GROUP * k
        if -m < GROUPS_PER_SLAB:
            r_lo, r_hi = pltpu.roll(lo, cut, axis=1), pltpu.roll(hi, cut, axis=1)
            out = [jnp.where(lane < cut, r_lo, r_hi), jnp.where(lane < cut, r_hi, zero)]
        elif k == 0:
            out = [hi, zero]
        else:
            out = [jnp.where(lane < cut, pltpu.roll(hi, cut, axis=1), zero), zero]
    return jnp.concatenate(out, axis=1)


def _s5prep_kernel(are_ref, aim_ref, ldt_ref, bre_ref, bim_ref, cre_ref, cim_ref,
                   w1_ref, cm_ref, a16re_ref, a16im_ref):
    ck = S5_CHUNK
    rows = lax.broadcasted_iota(jnp.int32, (S5_ROW, 1), 0) // S5_GROUP
    fwd = lax.broadcasted_iota(jnp.int32, (1, 2 * S5_STATE), 1) < S5_STATE
    dn = (((1,), (1,)), ((), ()))
    hi = lax.Precision.HIGHEST
    for g in range(GROUPS_PER_SLAB):
        are = are_ref[0, g:g + 1, :]
        aim = aim_ref[0, g:g + 1, :]
        dt = jnp.exp(ldt_ref[0, g:g + 1, :])

        p_col = lax.broadcasted_iota(jnp.int32, (POWER_ROWS, 1), 0).astype(F32)
        mag = jnp.exp(p_col * (dt * are))
        tab_re = mag * jnp.cos(p_col * (dt * aim))
        tab_im = mag * jnp.sin(p_col * (dt * aim))

        def power(pidx):
            acc_re = jnp.broadcast_to(tab_re[0:1, :], pidx.shape)
            acc_im = jnp.broadcast_to(tab_im[0:1, :], pidx.shape)
            for p in range(1, ck + 1):
                hit = pidx == p
                acc_re = jnp.where(hit, tab_re[p:p + 1, :], acc_re)
                acc_im = jnp.where(hit, tab_im[p:p + 1, :], acc_im)
            return acc_re, acc_im

        abre, abim = tab_re[1:2, :], tab_im[1:2, :]
        den = are * are + aim * aim
        nr = abre - 1.0
        fre = (nr * are + abim * aim) / den
        fim = (abim * are - nr * aim) / den
        b_re, b_im = bre_ref[0, g], bim_ref[0, g]
        bb_re = fre * b_re - fim * b_im
        bb_im = fre * b_im + fim * b_re
        tile = lambda t: jnp.concatenate([t] * ck, axis=0)
        step = SUBLANES * (rows // SUBLANES) + ((rows % SUBLANES - g) & (SUBLANES - 1))
        p_re, p_im = power(jnp.where(fwd, ck - 1 - step, step))
        t_re, t_im = tile(bb_re), tile(bb_im)
        w1_ref[0, g, :, S5_ROW:S5_ROW + 2 * S5_STATE] = (p_re * t_re - p_im * t_im).astype(BF16)
        w1_ref[0, g, :, S5_ROW + 2 * S5_STATE:] = (p_re * t_im + p_im * t_re).astype(BF16)
        c_re, c_im = tile(cre_ref[0, g]), tile(cim_ref[0, g])
        p_re, p_im = power(jnp.where(fwd, step + 1, ck - step))
        cm_ref[0, g, :2 * S5_STATE, :] = (c_re * p_re - c_im * p_im).T.astype(BF16)
        cm_ref[0, g, 2 * S5_STATE:, :] = (-(c_re * p_im + c_im * p_re)).T.astype(BF16)
        p_re, p_im = power(jnp.where(fwd, rows, ck - 1 - rows))
        x_re = p_re * c_re - p_im * c_im
        x_im = p_re * c_im + p_im * c_re

        def kernels(mask):
            return (lax.dot_general(jnp.where(mask, bb_re, 0.0), x_re, dn, precision=hi, preferred_element_type=F32)
                    - lax.dot_general(jnp.where(mask, bb_im, 0.0), x_im, dn, precision=hi,
                                      preferred_element_type=F32))

        kf = kernels(fwd)
        kb = kernels(jnp.logical_not(fwd))
        for pos in range(ck):
            s = SUBLANES * (pos // SUBLANES) + (pos % SUBLANES - g) % SUBLANES
            blockrow = _shift_blocks(kf, s) + _shift_blocks(kb, s - (ck - 1))
            halves = [blockrow[:, :LANES], blockrow[:, LANES:]]
            if g:
                halves = [pltpu.roll(t, S5_GROUP * g, axis=1) for t in halves]
            w1_ref[0, g, pos * S5_GROUP:(pos + 1) * S5_GROUP, :S5_ROW] = jnp.concatenate(halves, axis=1).astype(BF16)
        a16re_ref[0, g:g + 1, :] = tab_re[ck:ck + 1, :]
        a16im_ref[0, g:g + 1, :] = tab_im[ck:ck + 1, :]


def _s5_operators(a_re, a_im, log_dt, b_re, b_im, c_re, c_im):
    depth = a_re.shape[0]
    n = depth * S5_SLABS
    gl = GROUPS_PER_SLAB
    vec = lambda t: jnp.moveaxis(t, 1, 2).reshape(n, gl, 2 * S5_STATE)
    ldt = jnp.broadcast_to(jnp.moveaxis(log_dt, 1, 2)[..., None], (depth, S5_GROUPS, 2, S5_STATE))
    bmat = lambda t: jnp.transpose(t, (0, 2, 4, 1, 3)).reshape(n, gl, S5_GROUP, 2 * S5_STATE)
    cmat = lambda t: jnp.transpose(t, (0, 2, 3, 1, 4)).reshape(n, gl, S5_GROUP, 2 * S5_STATE)
    vspec = pl.BlockSpec((1, gl, 2 * S5_STATE), lambda m: (m, 0, 0))
    mspec = pl.BlockSpec((1, gl, S5_GROUP, 2 * S5_STATE), lambda m: (m, 0, 0, 0))
    w1, cm, a16re, a16im = pl.pallas_call(
        _s5prep_kernel,
        grid=(n,),
        in_specs=[vspec, vspec, vspec, mspec, mspec, mspec, mspec],
        out_specs=[pl.BlockSpec((1, gl, S5_ROW, S5_ROW + 4 * S5_STATE), lambda m: (m, 0, 0, 0)),
                   pl.BlockSpec((1, gl, 4 * S5_STATE, S5_ROW), lambda m: (m, 0, 0, 0)), vspec, vspec],
        out_shape=[jax.ShapeDtypeStruct((n, gl, S5_ROW, S5_ROW + 4 * S5_STATE), BF16),
                   jax.ShapeDtypeStruct((n, gl, 4 * S5_STATE, S5_ROW), BF16),
                   jax.ShapeDtypeStruct((n, gl, 2 * S5_STATE), F32), jax.ShapeDtypeStruct((n, gl, 2 * S5_STATE), F32)],
        compiler_params=_cparams(1),
        name="s5_operators",
    )(vec(a_re), vec(a_im), ldt.reshape(n, gl, 2 * S5_STATE), bmat(b_re), bmat(b_im), cmat(c_re), cmat(c_im))
    r = lambda t: t.reshape((depth, S5_SLABS) + t.shape[1:])
    return dict(w1=r(w1), cm=r(cm), are=r(a16re), aim=r(a16im))


def _s5_kernel(z_ref, w1_ref, cm_ref, are_ref, aim_ref, d_ref, y_ref,
               u_all, yin, lre, lim, sre_f, sre_b, sim_f, sim_b, *, n_chunks, n_ctx_chunks, pitch):
    ng = GROUPS_PER_SLAB
    rb_rows = S5_ASM_ROWS
    blk = lax.broadcasted_iota(jnp.int32, (rb_rows, LANES), 1) // S5_GROUP

    def step_rows(rb, j):
        return pl.ds(rb * rb_rows * S5_CHUNK + j, rb_rows, stride=S5_CHUNK)

    for rb in range(n_chunks // rb_rows):
        rows = slice(rb * rb_rows, (rb + 1) * rb_rows)
        for h in range(2):
            rot = []
            for s in range(SUBLANES):
                zj = z_ref[step_rows(rb, SUBLANES * h + s), :]
                rot.append(zj if s == 0 else pltpu.roll(zj, S5_GROUP * s, axis=1))
            for g in range(ng):
                u = rot[(0 - g) % ng]
                for q in range(1, ng):
                    u = jnp.where(blk == q, rot[(q - g) % ng], u)
                u_all[g, rows, h * LANES:(h + 1) * LANES] = u.astype(BF16)
    for g in range(ng):
        r = _dot(u_all[g], w1_ref[g])
        yin[g] = r[:, :S5_ROW]
        lre[pl.ds(g * pitch, n_chunks), :] = r[:, S5_ROW:S5_ROW + 2 * S5_STATE]
        lim[pl.ds(g * pitch, n_chunks), :] = r[:, S5_ROW + 2 * S5_STATE:]

    ar = are_ref[...]
    ai = aim_ref[...]
    fwd_lane = lax.broadcasted_iota(jnp.int32, (ng, 2 * S5_STATE), 1) < S5_STATE

    def body(kstep, carry):
        st_re, st_im = carry
        rb = jnp.where(kstep < n_ctx_chunks, n_ctx_chunks - 1 - kstep, n_chunks - 1 + n_ctx_chunks - kstep)
        rows_f = pl.ds(kstep, ng, stride=pitch)
        rows_b = pl.ds(rb, ng, stride=pitch)
        sre_f[rows_f, :] = st_re
        sre_b[rows_b, :] = st_re
        sim_f[rows_f, :] = st_im
        sim_b[rows_b, :] = st_im
        loc_re = jnp.where(fwd_lane, lre[rows_f, :], lre[rows_b, :])
        loc_im = jnp.where(fwd_lane, lim[rows_f, :], lim[rows_b, :])
        return ar * st_re - ai * st_im + loc_re, ar * st_im + ai * st_re + loc_im

    zero = jnp.zeros((ng, 2 * S5_STATE), F32)
    lax.fori_loop(0, n_chunks, body, (zero, zero), unroll=SCAN_UNROLL)

    fwd_big = lax.broadcasted_iota(jnp.int32, (n_chunks, 2 * S5_STATE), 1) < S5_STATE
    for g in range(ng):
        rows = pl.ds(g * pitch, n_chunks)
        s_re = jnp.where(fwd_big, sre_f[rows, :], sre_b[rows, :]).astype(BF16)
        s_im = jnp.where(fwd_big, sim_f[rows, :], sim_b[rows, :]).astype(BF16)
        yin[g] = yin[g] + _dot(jnp.concatenate([s_re, s_im], axis=1), cm_ref[g])

    d = d_ref[...]
    for rb in range(n_chunks // rb_rows):
        rows = slice(rb * rb_rows, (rb + 1) * rb_rows)
        for h in range(2):
            ys = [yin[g, rows, h * LANES:(h + 1) * LANES] for g in range(ng)]
            for s in range(SUBLANES):
                v = ys[(0 - s) % ng]
                for q in range(1, ng):
                    v = jnp.where(blk == q, ys[(q - s) % ng], v)
                if s:
                    v = pltpu.roll(v, LANES - S5_GROUP * s, axis=1)
                tok = step_rows(rb, SUBLANES * h + s)
                y_ref[tok, :] = v + d * z_ref[tok, :]


def _s5_call(s5u, ops, s5_d, layer, nb, n_chunks, n_ctx_chunks):
    t = s5u.shape[0]
    s = t // nb
    pitch = -(-n_chunks // SUBLANES) * SUBLANES
    if (pitch // SUBLANES) % 2 == 0:
        pitch += SUBLANES
    gl = GROUPS_PER_SLAB
    mat = lambda r, c: pl.BlockSpec((None, None, gl, r, c), lambda a, b: (layer, a, 0, 0, 0))
    vec = pl.BlockSpec((None, None, gl, 2 * S5_STATE), lambda a, b: (layer, a, 0, 0))
    state = pltpu.VMEM((gl * pitch, 2 * S5_STATE), F32)
    return pl.pallas_call(
        functools.partial(_s5_kernel, n_chunks=n_chunks, n_ctx_chunks=n_ctx_chunks, pitch=pitch),
        grid=(S5_SLABS, nb),
        in_specs=[pl.BlockSpec((s, LANES), lambda a, b: (b, a)),
                  mat(S5_ROW, S5_ROW + 4 * S5_STATE), mat(4 * S5_STATE, S5_ROW),
                  vec, vec, pl.BlockSpec((1, LANES), lambda a, b: (0, a))],
        out_specs=pl.BlockSpec((s, LANES), lambda a, b: (b, a)),
        out_shape=jax.ShapeDtypeStruct((t, S5_WIDTH), F32),
        scratch_shapes=[pltpu.VMEM((gl, n_chunks, S5_ROW), BF16), pltpu.VMEM((gl, n_chunks, S5_ROW), F32)] + [state] * 6,
        compiler_params=_cparams(2),
        name="s5_scan",
    )(s5u, ops["w1"], ops["cm"], ops["are"], ops["aim"], s5_d)


def _rot_half_lanes(x, first_half):
    return jnp.where(first_half, -pltpu.roll(x, LANES - ROPE_HALF, axis=1), pltpu.roll(x, ROPE_HALF, axis=1))


def _inproj_kernel(ctx_ref, lat_ref, mod_ref, g_ref, w_ref, qn_ref, kvn_ref, wq_ref, wkv_ref, rope_ref,
                   qc_out, ql_out, k_out, v_out, cb_out, uc_out, s5_out, gate_out):
    mod = mod_ref[0]
    x = jnp.where(pl.program_id(1) == 0, ctx_ref[...], lat_ref[...])
    xn = _rms(x, g_ref[...]) * (1.0 + mod[:, D_MODEL:]) + mod[:, :D_MODEL]
    xn = xn.astype(BF16)
    nq = N_HEADS * HEAD_PAD

    def proj(a, n):
        return _dot(xn, w_ref[:, a:a + n])

    def gate(c):
        gate_out[:, c * D_MODEL:(c + 1) * D_MODEL] = _sigmoid(proj(P_G + c * D_MODEL, D_MODEL)).astype(BF16)

    zqkv = proj(P_QKV, Q_LORA + KV_LORA)
    gate(0)
    qn = _rms(zqkv[:, :Q_LORA], qn_ref[...]).astype(BF16)
    kvn = _rms(zqkv[:, Q_LORA:], kvn_ref[...]).astype(BF16)
    q1 = _dot(qn, wq_ref[...])
    kv = _dot(kvn, wkv_ref[...])
    zpe = proj(P_PE, LANES)
    gate(1)
    rope = rope_ref[...]
    cos_q, sin_q = rope[:, 0:LANES], rope[:, LANES:2 * LANES]
    cos_k, sin_k = rope[:, 2 * LANES:3 * LANES], rope[:, 3 * LANES:4 * LANES]
    first_half = lax.broadcasted_iota(jnp.int32, (1, LANES), 1) % (2 * ROPE_HALF) < ROPE_HALF
    kpe = zpe * cos_k + _rot_half_lanes(zpe, first_half) * sin_k
    kslot = pltpu.roll(kpe, QK_NOPE, axis=1)
    q_heads, k_heads = [], []
    for h in range(N_HEADS):
        sl = slice(h * HEAD_PAD, (h + 1) * HEAD_PAD)
        qh = q1[:, sl]
        q_heads.append(((qh * cos_q + _rot_half_lanes(qh, first_half) * sin_q) * Q_SCALE).astype(BF16))
        k_heads.append((kv[:, sl] + kslot).astype(BF16))
    q = jnp.concatenate(q_heads, axis=1)
    k_out[...] = jnp.concatenate(k_heads, axis=1)
    qc_out[...] = q
    ql_out[...] = q
    one_lane = lax.broadcasted_iota(jnp.int32, (1, nq), 1) % HEAD_PAD == V_DIM
    v_out[...] = (kv[:, nq:] + jnp.where(one_lane, 1.0, 0.0)).astype(BF16)
    gate(2)
    cb_out[...] = proj(P_CB, CONV_WIDTH).astype(BF16)
    uc_out[...] = (proj(P_CC, CONV_WIDTH) * proj(P_CX, CONV_WIDTH)).astype(BF16)
    s5_out[...] = proj(P_S5, S5_WIDTH)


def _token_specs(src, first_tile, width=D_MODEL):
    return [pl.BlockSpec((TM, width), lambda b, i: (b * src.ctx_stride, 0)),
            pl.BlockSpec((TM, width),
                         lambda b, i: (b * src.lat_stride + src.lat_off + jnp.maximum(i + first_tile - 1, 0), 0))]


def _inproj_call(src, mods, lw, rope, nb, tps):
    t = nb * tps * TM
    rowb = lambda n: pl.BlockSpec((TM, n), lambda b, i: (b * tps + i, 0))
    nq = N_HEADS * HEAD_PAD
    outs = [(nq, BF16), (nq, BF16), (CONV_WIDTH, BF16), (CONV_WIDTH, BF16), (S5_WIDTH, F32),
            (N_BRANCH * D_MODEL, BF16)]
    q_specs = [pl.BlockSpec((TM, nq), lambda b, i: (jnp.where(i == 0, b, nb + b), 0)),
               pl.BlockSpec((TM, nq), lambda b, i: (b * (tps - 1) + jnp.maximum(i - 1, 0), 0))]
    q_shapes = [jax.ShapeDtypeStruct((2 * nb * TM, nq), BF16),
                jax.ShapeDtypeStruct((nb * (tps - 1) * TM, nq), BF16)]
    return pl.pallas_call(
        _inproj_kernel,
        grid=(nb, tps),
        in_specs=_token_specs(src, 0) + [
                  pl.BlockSpec((1, 1, 2 * D_MODEL), lambda b, i: (jnp.where(i == 0, nb, b), 0, 0)),
                  _const_spec((1, D_MODEL)),
                  _const_spec((D_MODEL, P_COLS)),
                  _const_spec((1, Q_LORA)), _const_spec((1, KV_LORA)),
                  _const_spec((Q_LORA, nq)), _const_spec((KV_LORA, 2 * nq)),
                  pl.BlockSpec((TM, 4 * LANES), lambda b, i: (i, 0))],
        out_specs=q_specs + [rowb(n) for n, _ in outs],
        out_shape=q_shapes + [jax.ShapeDtypeStruct((t, n), dt) for n, dt in outs],
        compiler_params=_cparams(2),
        name="in_projection",
    )(src.ctx, src.lat, mods, lw["norm_mix"], lw["w_in"], lw["q_norm"], lw["kv_norm"], lw["wq"], lw["wkv"],
      rope)


def _attn_head(q, k, v):
    s = lax.dot_general(q, k, (((1,), (1,)), ((), ())), preferred_element_type=F32)
    p = jnp.exp2(s - jnp.max(s, axis=-1, keepdims=True)).astype(BF16)
    oe = _dot(p, v)
    return oe[:, :V_DIM] / oe[:, V_DIM:V_DIM + 1]


def _attn_kernel(q_ref, k_ref, v_ref, o_ref):
    outs = []
    for h in range(N_HEADS):
        sl = slice(h * HEAD_PAD, (h + 1) * HEAD_PAD)
        outs.append(_attn_head(q_ref[:, sl], k_ref[:, sl], v_ref[:, sl]))
    o_ref[...] = jnp.concatenate(outs, axis=-1).astype(BF16)


def _attn_call(q, k, v, nb, n_q, rows_q, rows_kv, kv_stride, name):
    nq = N_HEADS * HEAD_PAD
    nv = N_HEADS * V_DIM
    kv_map = lambda b, i: (b * (kv_stride // rows_kv), 0)
    kv_spec = pl.BlockSpec((rows_kv, nq), kv_map)
    return pl.pallas_call(
        _attn_kernel,
        grid=(nb, n_q),
        in_specs=[pl.BlockSpec((rows_q, nq), lambda b, i: (b * n_q + i, 0)), kv_spec, kv_spec],
        out_specs=pl.BlockSpec((rows_q, nv), lambda b, i: (b * n_q + i, 0)),
        out_shape=jax.ShapeDtypeStruct((nb * n_q * rows_q, nv), BF16),
        compiler_params=_cparams(2),
        name=name,
    )(q, k, v)


def _gelu_tanh(x):
    return 0.5 * x * (1.0 + jnp.tanh(math.sqrt(2.0 / math.pi) * (x + 0.044715 * (x * x * x))))


def _merge_mlp_kernel(ctx_ref, lat_ref, moda_ref, modm_ref, oc_ref, ol_ref, cb_ref, uc_ref, ucp_ref, ucn_ref, ys_ref,
                      gate_ref, wo_ref, cw_ref, cwo_ref, wglu_ref, wout_ref, gm_ref, w1_ref, w2_ref, gf_ref, out_ref,
                      *, first_tile, tps, final):
    i = pl.program_id(1) + first_tile
    uc = uc_ref[...].astype(F32)
    prev_row = jnp.where(i >= 2, ucp_ref[HALO - 1:HALO, :].astype(F32), 0.0)
    next_row = jnp.where(jnp.logical_and(i >= 1, i < tps - 1), ucn_ref[0:1, :].astype(F32), 0.0)
    row = lax.broadcasted_iota(jnp.int32, (TM, 1), 0)
    up = jnp.where(row == 0, prev_row, pltpu.roll(uc, 1, axis=0))
    dn = jnp.where(row == TM - 1, next_row, pltpu.roll(uc, TM - 1, axis=0))
    y = up * cw_ref[0:1, :] + uc * cw_ref[1:2, :] + dn * cw_ref[2:3, :]
    att = _dot(jnp.where(i == 0, oc_ref[...], ol_ref[...]), wo_ref[...])
    conv = _dot((cb_ref[...].astype(F32) * y).astype(BF16), cwo_ref[...])
    z = _dot(_gelu_tanh(ys_ref[...]).astype(BF16), wglu_ref[...])
    s5o = z[:, :D_MODEL] * _sigmoid(z[:, D_MODEL:])
    g = gate_ref[...].astype(F32)
    merged = g[:, :D_MODEL] * att + g[:, D_MODEL:2 * D_MODEL] * conv + g[:, 2 * D_MODEL:] * s5o
    x = jnp.where(i == 0, ctx_ref[...], lat_ref[...])
    x = x + moda_ref[0] * _dot(merged.astype(BF16), wout_ref[...])
    mod = modm_ref[0]
    h = (_rms(x, gm_ref[...]) * (1.0 + mod[:, D_MODEL:2 * D_MODEL]) + mod[:, :D_MODEL]).astype(BF16)
    acc = jnp.zeros((TM, D_MODEL), F32)
    for c in range(D_FF // D_MODEL):
        a = jnp.maximum(_dot(h, w1_ref[:, c * D_MODEL:(c + 1) * D_MODEL]), 0.0)
        acc = acc + _dot((a * a).astype(BF16), w2_ref[c * D_MODEL:(c + 1) * D_MODEL, :])
    y = x + mod[:, 2 * D_MODEL:] * acc
    if final:
        y = _rms(y, gf_ref[...])
    out_ref[...] = y


def _merge_mlp_call(src, mods, o_src, cb, uc, ys, gate, lw, norm_final, nb, tps, first_tile, final):
    t = cb.shape[0]
    n_tiles = tps - first_tile
    blk = lambda b, i: b * tps + i + first_tile
    rowb = lambda n: pl.BlockSpec((TM, n), lambda b, i: (blk(b, i), 0))
    mod_row = lambda b, i: jnp.where(i + first_tile == 0, nb, b)
    per_halo = TM // HALO
    last_halo = t // HALO - 1
    return pl.pallas_call(
        functools.partial(_merge_mlp_kernel, first_tile=first_tile, tps=tps, final=final),
        grid=(nb, n_tiles),
        in_specs=_token_specs(src, first_tile) + [
                  pl.BlockSpec((1, 1, D_MODEL), lambda b, i: (mod_row(b, i), 0, 2)),
                  pl.BlockSpec((1, 1, 3 * D_MODEL), lambda b, i: (mod_row(b, i), 0, 1))]
                 + _token_specs(o_src, first_tile, N_HEADS * V_DIM) + [
                  rowb(CONV_WIDTH), rowb(CONV_WIDTH),
                  pl.BlockSpec((HALO, CONV_WIDTH), lambda b, i: (jnp.maximum(blk(b, i) * per_halo - 1, 0), 0)),
                  pl.BlockSpec((HALO, CONV_WIDTH),
                               lambda b, i: (jnp.minimum((blk(b, i) + 1) * per_halo, last_halo), 0)),
                  rowb(S5_WIDTH), rowb(N_BRANCH * D_MODEL),
                  _const_spec((N_HEADS * V_DIM, D_MODEL)), _const_spec((CONV_K, CONV_WIDTH)),
                  _const_spec((CONV_WIDTH, D_MODEL)),
                  _const_spec((S5_WIDTH, 2 * D_MODEL)), _const_spec((D_MODEL, D_MODEL)),
                  _const_spec((1, D_MODEL)), _const_spec((D_MODEL, D_FF)), _const_spec((D_FF, D_MODEL)),
                  _const_spec((1, D_MODEL))],
        out_specs=pl.BlockSpec((TM, D_MODEL), lambda b, i: (b * n_tiles + i, 0)),
        out_shape=jax.ShapeDtypeStruct((nb * n_tiles * TM, D_MODEL), F32),
        compiler_params=_cparams(2),
        name="merge_mlp",
    )(src.ctx, src.lat, mods, mods, o_src.ctx, o_src.lat, cb, uc, uc, uc, ys, gate,
      lw["w_o"], lw["conv_w"], lw["conv_w_out"], lw["w_glu"], lw["w_out"], lw["norm_mlp"], lw["w1"], lw["w2"],
      norm_final)


def _layer_weights(i, w_in, norm_mix, q_norm, w_uq, kv_norm, w_ukv, w_o, conv_w, conv_w_out, s5_d, w_glu,
                   w_out, norm_mlp, w1, w2):
    wi = w_in[i]
    pe = wi[:, OFF_PE:OFF_CB]
    w_in_p = jnp.concatenate(
        [wi[:, OFF_Q:OFF_PE], wi[:, OFF_CB:OFF_G], wi[:, OFF_G:], pe,
         jnp.zeros((D_MODEL, LANES - QK_ROPE), F32)], axis=1).astype(BF16)
    uq = w_uq[i].reshape(Q_LORA, N_HEADS, QK_NOPE + QK_ROPE)
    zpad = jnp.zeros((Q_LORA, N_HEADS, HEAD_PAD - QK_NOPE - QK_ROPE), F32)
    wq = jnp.concatenate([uq, zpad], axis=-1).reshape(Q_LORA, N_HEADS * HEAD_PAD)
    ukv = w_ukv[i].reshape(KV_LORA, N_HEADS, QK_NOPE + V_DIM)
    wk = jnp.concatenate([ukv[..., :QK_NOPE], jnp.zeros((KV_LORA, N_HEADS, HEAD_PAD - QK_NOPE), F32)],
                         axis=-1).reshape(KV_LORA, N_HEADS * HEAD_PAD)
    wv = jnp.concatenate([ukv[..., QK_NOPE:], jnp.zeros((KV_LORA, N_HEADS, HEAD_PAD - V_DIM), F32)],
                         axis=-1).reshape(KV_LORA, N_HEADS * HEAD_PAD)
    return dict(
        w_in=w_in_p, norm_mix=norm_mix[i].reshape(1, D_MODEL),
        q_norm=q_norm[i].reshape(1, Q_LORA), kv_norm=kv_norm[i].reshape(1, KV_LORA),
        wq=wq.astype(BF16),
        wkv=jnp.concatenate([wk, wv], axis=1).astype(BF16),
        w_o=w_o[i].astype(BF16), conv_w=conv_w[i], conv_w_out=conv_w_out[i].astype(BF16),
        s5_d=s5_d[i].reshape(1, S5_WIDTH), w_glu=w_glu[i].astype(BF16), w_out=w_out[i].astype(BF16),
        norm_mlp=norm_mlp[i].reshape(1, D_MODEL), w1=w1[i].astype(BF16), w2=w2[i].astype(BF16))


def _rope_tables(n_ctx, n_tokens):
    rows = n_tokens // GRID_W
    pos = jnp.stack([jnp.repeat(jnp.arange(rows), GRID_W), jnp.tile(jnp.arange(GRID_W), rows)], -1).astype(F32)
    n_freq = QK_ROPE // 4
    inv = ROPE_THETA ** (-jnp.arange(n_freq, dtype=F32) / n_freq)
    ang = pos[:, :, None, None] * inv[None, None, None, :]
    ang = jnp.broadcast_to(ang, (n_tokens, 2, 2, n_freq)).reshape(n_tokens, QK_ROPE)
    cos = jnp.concatenate([jnp.ones((n_ctx, QK_ROPE), F32), jnp.cos(ang)], axis=0)
    sin = jnp.concatenate([jnp.zeros((n_ctx, QK_ROPE), F32), jnp.sin(ang)], axis=0)
    s = n_ctx + n_tokens
    one = jnp.ones((s, QK_NOPE), F32)
    z = lambda n: jnp.zeros((s, n), F32)
    pad = HEAD_PAD - QK_NOPE - QK_ROPE
    return jnp.concatenate([one, cos, z(pad), z(QK_NOPE), sin, z(pad),
                            cos, z(LANES - QK_ROPE), sin, z(LANES - QK_ROPE)], axis=1)


def kernel(x, c, ctx, c_ctx, ada_w, ada_b, norm_mix, w_in, mla_q_norm, mla_w_uq, mla_kv_norm, mla_w_ukv, mla_w_o, conv_w, conv_w_out, s5_a_re, s5_a_im, s5_log_dt, s5_b_re, s5_b_im, s5_c_re, s5_c_im, s5_d, s5_w_glu, w_out, norm_mlp, mlp_w1, mlp_w2, norm_final):
    nb, n_lat, _ = x.shape
    n_ctx = ctx.shape[1]
    depth = ada_w.shape[0]
    assert n_ctx == TM and n_lat % TQ == 0 and n_lat % GRID_W == 0 and nb == SUBLANES
    assert (n_ctx + n_lat) % (S5_CHUNK * S5_ASM_ROWS) == 0
    s = n_ctx + n_lat
    tps = s // TM
    t = nb * s
    n_chunks = s // S5_CHUNK
    n_ctx_chunks = n_ctx // S5_CHUNK

    c16 = jnp.zeros((16, D_MODEL), F32).at[:nb].set(c).at[nb].set(c_ctx)
    mods = _ada_call(c16, ada_w, ada_b)[:, :nb + 1].reshape(depth, nb + 1, 1, N_MOD * D_MODEL)
    ops = _s5_operators(s5_a_re, s5_a_im, s5_log_dt, s5_b_re, s5_b_im, s5_c_re, s5_c_im)
    rope = _rope_tables(n_ctx, n_lat)
    src = _TokenSource(ctx.reshape(nb * n_ctx, D_MODEL), x.reshape(nb * n_lat, D_MODEL), 1, tps - 1, 0)
    gf = norm_final.reshape(1, D_MODEL)

    for i in range(depth):
        last = i == depth - 1
        ft = 1 if last else 0
        lw = _layer_weights(i, w_in, norm_mix, mla_q_norm, mla_w_uq, mla_kv_norm, mla_w_ukv, mla_w_o, conv_w,
                            conv_w_out, s5_d, s5_w_glu, w_out, norm_mlp, mlp_w1, mlp_w2)
        q_ctx, q_lat, k, v, cb, uc, s5u, gate = _inproj_call(src, mods[i], lw, rope, nb, tps)
        o_lat = _attn_call(q_lat, k, v, nb, n_lat // TQ, TQ, s, s, "attention")
        o_ctx = o_lat if last else _attn_call(q_ctx, k, v, nb, 1, TM, n_ctx, s, "attention_ctx")
        o_src = _TokenSource(o_ctx, o_lat, 1, tps - 1, 0)
        ys = _s5_call(s5u, ops, lw["s5_d"], i, nb, n_chunks, n_ctx_chunks)
        xs = _merge_mlp_call(src, mods[i], o_src, cb, uc, ys, gate, lw, gf, nb, tps, ft, last)
        src = _TokenSource(xs, xs, tps, tps, 1)
    return xs.reshape(nb, n_lat, D_MODEL)
```

```python
import functools
import math
from typing import NamedTuple

import jax
import jax.numpy as jnp
from jax import lax
from jax.experimental import pallas as pl
from jax.experimental.pallas import tpu as pltpu

F32 = jnp.float32
BF16 = jnp.bfloat16

D_MODEL = 1024
GRID_W = 64
N_HEADS = 8
QK_NOPE = 64
QK_ROPE = 32
V_DIM = 64
Q_LORA = 256
KV_LORA = 256
ROPE_THETA = 10000.0
ROPE_HALF = QK_ROPE // 4
ATTN_SCALE = 1.0 / math.sqrt(QK_NOPE + QK_ROPE)
Q_SCALE = ATTN_SCALE * math.log2(math.e)
CONV_WIDTH = 512
CONV_K = 3
S5_WIDTH = 512
S5_GROUP = 16
S5_GROUPS = S5_WIDTH // S5_GROUP
S5_STATE = 64
N_BRANCH = 3
D_FF = 4 * D_MODEL
EPS = 1e-6
N_MOD = 6

OFF_Q = 0
OFF_KV = OFF_Q + Q_LORA
OFF_PE = OFF_KV + KV_LORA
OFF_CB = OFF_PE + QK_ROPE
OFF_CC = OFF_CB + CONV_WIDTH
OFF_CX = OFF_CC + CONV_WIDTH
OFF_S5 = OFF_CX + CONV_WIDTH
OFF_G = OFF_S5 + S5_WIDTH
IN_COLS = OFF_G + N_BRANCH * D_MODEL

LANES = 128
SUBLANES = 8
HEAD_PAD = LANES

P_QKV = 0
P_CB = P_QKV + Q_LORA + KV_LORA
P_CC = P_CB + CONV_WIDTH
P_CX = P_CC + CONV_WIDTH
P_S5 = P_CX + CONV_WIDTH
P_G = P_S5 + S5_WIDTH
P_PE = P_G + N_BRANCH * D_MODEL
P_COLS = P_PE + LANES

TM = 256
TQ = 512
ATTN_STEP_ROWS = 2 * TQ
HALO = 16
S5_CHUNK = 16
S5_ROW = S5_CHUNK * S5_GROUP
GROUPS_PER_SLAB = LANES // S5_GROUP
S5_SLABS = S5_WIDTH // LANES
SCAN_UNROLL = 4
S5_ASM_ROWS = 16
POWER_ROWS = 24
VMEM_LIMIT = 56 * 1024 * 1024


class _TokenSource(NamedTuple):
    ctx: jax.Array
    lat: jax.Array
    ctx_stride: int
    lat_stride: int
    lat_off: int


def _cparams(n_axes):
    return pltpu.CompilerParams(dimension_semantics=("arbitrary",) * n_axes,
                                vmem_limit_bytes=VMEM_LIMIT)


def _const_spec(shape):
    nd = len(shape)
    return pl.BlockSpec(shape, lambda *_: (0,) * nd, pipeline_mode=pl.Buffered(1))


def _rms(x, g):
    return x * lax.rsqrt(jnp.mean(x * x, axis=-1, keepdims=True) + EPS) * g


def _sigmoid(x):
    return 1.0 / (1.0 + jnp.exp(-x))


def _dot(a, b):
    return jnp.dot(a, b, preferred_element_type=F32)


def _ada_kernel(c_ref, w_ref, b_ref, o_ref):
    c = c_ref[...]
    s = (c * _sigmoid(c)).astype(BF16)
    o_ref[0] = _dot(s, w_ref[0].astype(BF16)) + b_ref[0]


def _ada_call(c16, ada_w, ada_b):
    depth = ada_w.shape[0]
    n_col = N_MOD * D_MODEL
    bn = n_col // 4
    return pl.pallas_call(
        _ada_kernel,
        grid=(depth, n_col // bn),
        in_specs=[pl.BlockSpec((16, D_MODEL), lambda l, j: (0, 0)),
                  pl.BlockSpec((1, D_MODEL, bn), lambda l, j: (l, 0, j)),
                  pl.BlockSpec((1, 1, bn), lambda l, j: (l, 0, j))],
        out_specs=pl.BlockSpec((1, 16, bn), lambda l, j: (l, 0, j)),
        out_shape=jax.ShapeDtypeStruct((depth, 16, n_col), F32),
        compiler_params=_cparams(2),
        name="ada_mod",
    )(c16, ada_w, ada_b.reshape(depth, 1, n_col))


def _shift_blocks(x, m):
    lo, hi = x[:, :LANES], x[:, LANES:]
    lane = lax.broadcasted_iota(jnp.int32, lo.shape, 1)
    zero = jnp.zeros_like(lo)
    if m == 0:
        return x
    k = abs(m) % GROUPS_PER_SLAB
    if m > 0:
        cut = S5_GROUP * k
        if m < GROUPS_PER_SLAB:
            r_lo, r_hi = pltpu.roll(lo, cut, axis=1), pltpu.roll(hi, cut, axis=1)
            out = [jnp.where(lane >= cut, r_lo, zero), jnp.where(lane >= cut, r_hi, r_lo)]
        elif k == 0:
            out = [zero, lo]
        else:
            out = [zero, jnp.where(lane >= cut, pltpu.roll(lo, cut, axis=1), zero)]
    else:
        cut = LANES - S5_GROUP * k
        if -m < GROUPS_PER_SLAB:
            r_lo, r_hi = pltpu.roll(lo, cut, axis=1), pltpu.roll(hi, cut, axis=1)
            out = [jnp.where(lane < cut, r_lo, r_hi), jnp.where(lane < cut, r_hi, zero)]
        elif k == 0:
            out = [hi, zero]
        else:
            out = [jnp.where(lane < cut, pltpu.roll(hi, cut, axis=1), zero), zero]
    return jnp.concatenate(out, axis=1)


def _s5prep_kernel(are_ref, aim_ref, ldt_ref, bre_ref, bim_ref, cre_ref, cim_ref,
                   w1_ref, cm_ref, a16re_ref, a16im_ref):
    ck = S5_CHUNK
    rows = lax.broadcasted_iota(jnp.int32, (S5_ROW, 1), 0) // S5_GROUP
    fwd = lax.broadcasted_iota(jnp.int32, (1, 2 * S5_STATE), 1) < S5_STATE
    dn = (((1,), (1,)), ((), ()))
    hi = lax.Precision.HIGHEST
    for g in range(GROUPS_PER_SLAB):
        are = are_ref[0, g:g + 1, :]
        aim = aim_ref[0, g:g + 1, :]
        dt = jnp.exp(ldt_ref[0, g:g + 1, :])

        p_col = lax.broadcasted_iota(jnp.int32, (POWER_ROWS, 1), 0).astype(F32)
        mag = jnp.exp(p_col * (dt * are))
        tab_re = mag * jnp.cos(p_col * (dt * aim))
        tab_im = mag * jnp.sin(p_col * (dt * aim))

        def power(pidx):
            acc_re = jnp.broadcast_to(tab_re[0:1, :], pidx.shape)
            acc_im = jnp.broadcast_to(tab_im[0:1, :], pidx.shape)
            for p in range(1, ck + 1):
                hit = pidx == p
                acc_re = jnp.where(hit, tab_re[p:p + 1, :], acc_re)
                acc_im = jnp.where(hit, tab_im[p:p + 1, :], acc_im)
            return acc_re, acc_im

        abre, abim = tab_re[1:2, :], tab_im[1:2, :]
        den = are * are + aim * aim
        nr = abre - 1.0
        fre = (nr * are + abim * aim) / den
        fim = (abim * are - nr * aim) / den
        b_re, b_im = bre_ref[0, g], bim_ref[0, g]
        bb_re = fre * b_re - fim * b_im
        bb_im = fre * b_im + fim * b_re
        tile = lambda t: jnp.concatenate([t] * ck, axis=0)
        step = SUBLANES * (rows // SUBLANES) + ((rows % SUBLANES - g) & (SUBLANES - 1))
        p_re, p_im = power(jnp.where(fwd, ck - 1 - step, step))
        t_re, t_im = tile(bb_re), tile(bb_im)
        w1_ref[0, g, :, S5_ROW:S5_ROW + 2 * S5_STATE] = (p_re * t_re - p_im * t_im).astype(BF16)
        w1_ref[0, g, :, S5_ROW + 2 * S5_STATE:] = (p_re * t_im + p_im * t_re).astype(BF16)
        c_re, c_im = tile(cre_ref[0, g]), tile(cim_ref[0, g])
        p_re, p_im = power(jnp.where(fwd, step + 1, ck - step))
        cm_ref[0, g, :2 * S5_STATE, :] = (c_re * p_re - c_im * p_im).T.astype(BF16)
        cm_ref[0, g, 2 * S5_STATE:, :] = (-(c_re * p_im + c_im * p_re)).T.astype(BF16)
        p_re, p_im = power(jnp.where(fwd, rows, ck - 1 - rows))
        x_re = p_re * c_re - p_im * c_im
        x_im = p_re * c_im + p_im * c_re

        def kernels(mask):
            return (lax.dot_general(jnp.where(mask, bb_re, 0.0), x_re, dn, precision=hi, preferred_element_type=F32)
                    - lax.dot_general(jnp.where(mask, bb_im, 0.0), x_im, dn, precision=hi,
                                      preferred_element_type=F32))

        kf = kernels(fwd)
        kb = kernels(jnp.logical_not(fwd))
        for pos in range(ck):
            s = SUBLANES * (pos // SUBLANES) + (pos % SUBLANES - g) % SUBLANES
            blockrow = _shift_blocks(kf, s) + _shift_blocks(kb, s - (ck - 1))
            halves = [blockrow[:, :LANES], blockrow[:, LANES:]]
            if g:
                halves = [pltpu.roll(t, S5_GROUP * g, axis=1) for t in halves]
            w1_ref[0, g, pos * S5_GROUP:(pos + 1) * S5_GROUP, :S5_ROW] = jnp.concatenate(halves, axis=1).astype(BF16)
        a16re_ref[0, g:g + 1, :] = tab_re[ck:ck + 1, :]
        a16im_ref[0, g:g + 1, :] = tab_im[ck:ck + 1, :]


def _s5_operators(a_re, a_im, log_dt, b_re, b_im, c_re, c_im):
    depth = a_re.shape[0]
    n = depth * S5_SLABS
    gl = GROUPS_PER_SLAB
    vec = lambda t: jnp.moveaxis(t, 1, 2).reshape(n, gl, 2 * S5_STATE)
    ldt = jnp.broadcast_to(jnp.moveaxis(log_dt, 1, 2)[..., None], (depth, S5_GROUPS, 2, S5_STATE))
    bmat = lambda t: jnp.transpose(t, (0, 2, 4, 1, 3)).reshape(n, gl, S5_GROUP, 2 * S5_STATE)
    cmat = lambda t: jnp.transpose(t, (0, 2, 3, 1, 4)).reshape(n, gl, S5_GROUP, 2 * S5_STATE)
    vspec = pl.BlockSpec((1, gl, 2 * S5_STATE), lambda m: (m, 0, 0))
    mspec = pl.BlockSpec((1, gl, S5_GROUP, 2 * S5_STATE), lambda m: (m, 0, 0, 0))
    w1, cm, a16re, a16im = pl.pallas_call(
        _s5prep_kernel,
        grid=(n,),
        in_specs=[vspec, vspec, vspec, mspec, mspec, mspec, mspec],
        out_specs=[pl.BlockSpec((1, gl, S5_ROW, S5_ROW + 4 * S5_STATE), lambda m: (m, 0, 0, 0)),
                   pl.BlockSpec((1, gl, 4 * S5_STATE, S5_ROW), lambda m: (m, 0, 0, 0)), vspec, vspec],
        out_shape=[jax.ShapeDtypeStruct((n, gl, S5_ROW, S5_ROW + 4 * S5_STATE), BF16),
                   jax.ShapeDtypeStruct((n, gl, 4 * S5_STATE, S5_ROW), BF16),
                   jax.ShapeDtypeStruct((n, gl, 2 * S5_STATE), F32), jax.ShapeDtypeStruct((n, gl, 2 * S5_STATE), F32)],
        compiler_params=_cparams(1),
        name="s5_operators",
    )(vec(a_re), vec(a_im), ldt.reshape(n, gl, 2 * S5_STATE), bmat(b_re), bmat(b_im), cmat(c_re), cmat(c_im))
    r = lambda t: t.reshape((depth, S5_SLABS) + t.shape[1:])
    return dict(w1=r(w1), cm=r(cm), are=r(a16re), aim=r(a16im))


def _s5_kernel(z_ref, w1_ref, cm_ref, are_ref, aim_ref, d_ref, y_ref,
               u_all, yin, lre, lim, sre_f, sre_b, sim_f, sim_b, *, n_chunks, n_ctx_chunks, pitch):
    ng = GROUPS_PER_SLAB
    rb_rows = S5_ASM_ROWS
    blk = lax.broadcasted_iota(jnp.int32, (rb_rows, LANES), 1) // S5_GROUP

    def step_rows(rb, j):
        return pl.ds(rb * rb_rows * S5_CHUNK + j, rb_rows, stride=S5_CHUNK)

    for rb in range(n_chunks // rb_rows):
        rows = slice(rb * rb_rows, (rb + 1) * rb_rows)
        for h in range(2):
            rot = []
            for s in range(SUBLANES):
                zj = z_ref[step_rows(rb, SUBLANES * h + s), :]
                rot.append(zj if s == 0 else pltpu.roll(zj, S5_GROUP * s, axis=1))
            for g in range(ng):
                u = rot[(0 - g) % ng]
                for q in range(1, ng):
                    u = jnp.where(blk == q, rot[(q - g) % ng], u)
                u_all[g, rows, h * LANES:(h + 1) * LANES] = u.astype(BF16)
    for g in range(ng):
        r = _dot(u_all[g], w1_ref[g])
        yin[g] = r[:, :S5_ROW]
        lre[pl.ds(g * pitch, n_chunks), :] = r[:, S5_ROW:S5_ROW + 2 * S5_STATE]
        lim[pl.ds(g * pitch, n_chunks), :] = r[:, S5_ROW + 2 * S5_STATE:]

    ar = are_ref[...]
    ai = aim_ref[...]
    fwd_lane = lax.broadcasted_iota(jnp.int32, (ng, 2 * S5_STATE), 1) < S5_STATE

    def body(kstep, carry):
        st_re, st_im = carry
        rb = jnp.where(kstep < n_ctx_chunks, n_ctx_chunks - 1 - kstep, n_chunks - 1 + n_ctx_chunks - kstep)
        rows_f = pl.ds(kstep, ng, stride=pitch)
        rows_b = pl.ds(rb, ng, stride=pitch)
        sre_f[rows_f, :] = st_re
        sre_b[rows_b, :] = st_re
        sim_f[rows_f, :] = st_im
        sim_b[rows_b, :] = st_im
        loc_re = jnp.where(fwd_lane, lre[rows_f, :], lre[rows_b, :])
        loc_im = jnp.where(fwd_lane, lim[rows_f, :], lim[rows_b, :])
        return ar * st_re - ai * st_im + loc_re, ar * st_im + ai * st_re + loc_im

    zero = jnp.zeros((ng, 2 * S5_STATE), F32)
    lax.fori_loop(0, n_chunks, body, (zero, zero), unroll=SCAN_UNROLL)

    fwd_big = lax.broadcasted_iota(jnp.int32, (n_chunks, 2 * S5_STATE), 1) < S5_STATE
    for g in range(ng):
        rows = pl.ds(g * pitch, n_chunks)
        s_re = jnp.where(fwd_big, sre_f[rows, :], sre_b[rows, :]).astype(BF16)
        s_im = jnp.where(fwd_big, sim_f[rows, :], sim_b[rows, :]).astype(BF16)
        yin[g] = yin[g] + _dot(jnp.concatenate([s_re, s_im], axis=1), cm_ref[g])

    d = d_ref[...]
    for rb in range(n_chunks // rb_rows):
        rows = slice(rb * rb_rows, (rb + 1) * rb_rows)
        for h in range(2):
            ys = [yin[g, rows, h * LANES:(h + 1) * LANES] for g in range(ng)]
            for s in range(SUBLANES):
                v = ys[(0 - s) % ng]
                for q in range(1, ng):
                    v = jnp.where(blk == q, ys[(q - s) % ng], v)
                if s:
                    v = pltpu.roll(v, LANES - S5_GROUP * s, axis=1)
                tok = step_rows(rb, SUBLANES * h + s)
                y_ref[tok, :] = v + d * z_ref[tok, :]


def _s5_call(s5u, ops, s5_d, layer, nb, n_chunks, n_ctx_chunks):
    t = s5u.shape[0]
    s = t // nb
    pitch = -(-n_chunks // SUBLANES) * SUBLANES
    if (pitch // SUBLANES) % 2 == 0:
        pitch += SUBLANES
    gl = GROUPS_PER_SLAB
    mat = lambda r, c: pl.BlockSpec((None, None, gl, r, c), lambda a, b: (layer, a, 0, 0, 0))
    vec = pl.BlockSpec((None, None, gl, 2 * S5_STATE), lambda a, b: (layer, a, 0, 0))
    state = pltpu.VMEM((gl * pitch, 2 * S5_STATE), F32)
    return pl.pallas_call(
        functools.partial(_s5_kernel, n_chunks=n_chunks, n_ctx_chunks=n_ctx_chunks, pitch=pitch),
        grid=(S5_SLABS, nb),
        in_specs=[pl.BlockSpec((s, LANES), lambda a, b: (b, a)),
                  mat(S5_ROW, S5_ROW + 4 * S5_STATE), mat(4 * S5_STATE, S5_ROW),
                  vec, vec, pl.BlockSpec((1, LANES), lambda a, b: (0, a))],
        out_specs=pl.BlockSpec((s, LANES), lambda a, b: (b, a)),
        out_shape=jax.ShapeDtypeStruct((t, S5_WIDTH), F32),
        scratch_shapes=[pltpu.VMEM((gl, n_chunks, S5_ROW), BF16), pltpu.VMEM((gl, n_chunks, S5_ROW), F32)] + [state] * 6,
        compiler_params=_cparams(2),
        name="s5_scan",
    )(s5u, ops["w1"], ops["cm"], ops["are"], ops["aim"], s5_d)


def _rot_half_lanes(x, first_half):
    return jnp.where(first_half, -pltpu.roll(x, LANES - ROPE_HALF, axis=1), pltpu.roll(x, ROPE_HALF, axis=1))


def _inproj_kernel(ctx_ref, lat_ref, mod_ref, g_ref, w_ref, qn_ref, kvn_ref, wq_ref, wkv_ref, rope_ref,
                   qc_out, ql_out, k_out, v_out, cb_out, uc_out, s5_out, gate_out):
    mod = mod_ref[0]
    x = jnp.where(pl.program_id(1) == 0, ctx_ref[...], lat_ref[...])
    xn = _rms(x, g_ref[...]) * (1.0 + mod[:, D_MODEL:]) + mod[:, :D_MODEL]
    xn = xn.astype(BF16)
    nq = N_HEADS * HEAD_PAD

    def proj(a, n):
        return _dot(xn, w_ref[:, a:a + n])

    def gate(c):
        gate_out[:, c * D_MODEL:(c + 1) * D_MODEL] = _sigmoid(proj(P_G + c * D_MODEL, D_MODEL)).astype(BF16)

    zqkv = proj(P_QKV, Q_LORA + KV_LORA)
    gate(0)
    qn = _rms(zqkv[:, :Q_LORA], qn_ref[...]).astype(BF16)
    kvn = _rms(zqkv[:, Q_LORA:], kvn_ref[...]).astype(BF16)
    q1 = _dot(qn, wq_ref[...])
    kv = _dot(kvn, wkv_ref[...])
    zpe = proj(P_PE, LANES)
    gate(1)
    rope = rope_ref[...]
    cos_q, sin_q = rope[:, 0:LANES], rope[:, LANES:2 * LANES]
    cos_k, sin_k = rope[:, 2 * LANES:3 * LANES], rope[:, 3 * LANES:4 * LANES]
    first_half = lax.broadcasted_iota(jnp.int32, (1, LANES), 1) % (2 * ROPE_HALF) < ROPE_HALF
    kpe = zpe * cos_k + _rot_half_lanes(zpe, first_half) * sin_k
    kslot = pltpu.roll(kpe, QK_NOPE, axis=1)
    q_heads, k_heads = [], []
    for h in range(N_HEADS):
        sl = slice(h * HEAD_PAD, (h + 1) * HEAD_PAD)
        qh = q1[:, sl]
        q_heads.append(((qh * cos_q + _rot_half_lanes(qh, first_half) * sin_q) * Q_SCALE).astype(BF16))
        k_heads.append((kv[:, sl] + kslot).astype(BF16))
    q = jnp.concatenate(q_heads, axis=1)
    k_out[...] = jnp.concatenate(k_heads, axis=1)
    qc_out[...] = q
    ql_out[...] = q
    one_lane = lax.broadcasted_iota(jnp.int32, (1, nq), 1) % HEAD_PAD == V_DIM
    v_out[...] = (kv[:, nq:] + jnp.where(one_lane, 1.0, 0.0)).astype(BF16)
    gate(2)
    cb_out[...] = proj(P_CB, CONV_WIDTH).astype(BF16)
    uc_out[...] = (proj(P_CC, CONV_WIDTH) * proj(P_CX, CONV_WIDTH)).astype(BF16)
    s5_out[...] = proj(P_S5, S5_WIDTH)


def _token_specs(src, first_tile, width=D_MODEL):
    return [pl.BlockSpec((TM, width), lambda b, i: (b * src.ctx_stride, 0)),
            pl.BlockSpec((TM, width),
                         lambda b, i: (b * src.lat_stride + src.lat_off + jnp.maximum(i + first_tile - 1, 0), 0))]


def _inproj_call(src, mods, lw, rope, nb, tps):
    t = nb * tps * TM
    rowb = lambda n: pl.BlockSpec((TM, n), lambda b, i: (b * tps + i, 0))
    nq = N_HEADS * HEAD_PAD
    outs = [(nq, BF16), (nq, BF16), (CONV_WIDTH, BF16), (CONV_WIDTH, BF16), (S5_WIDTH, F32),
            (N_BRANCH * D_MODEL, BF16)]
    q_specs = [pl.BlockSpec((TM, nq), lambda b, i: (jnp.where(i == 0, b, nb + b), 0)),
               pl.BlockSpec((TM, nq), lambda b, i: (b * (tps - 1) + jnp.maximum(i - 1, 0), 0))]
    q_shapes = [jax.ShapeDtypeStruct((2 * nb * TM, nq), BF16),
                jax.ShapeDtypeStruct((nb * (tps - 1) * TM, nq), BF16)]
    return pl.pallas_call(
        _inproj_kernel,
        grid=(nb, tps),
        in_specs=_token_specs(src, 0) + [
                  pl.BlockSpec((1, 1, 2 * D_MODEL), lambda b, i: (jnp.where(i == 0, nb, b), 0, 0)),
                  _const_spec((1, D_MODEL)),
                  _const_spec((D_MODEL, P_COLS)),
                  _const_spec((1, Q_LORA)), _const_spec((1, KV_LORA)),
                  _const_spec((Q_LORA, nq)), _const_spec((KV_LORA, 2 * nq)),
                  pl.BlockSpec((TM, 4 * LANES), lambda b, i: (i, 0))],
        out_specs=q_specs + [rowb(n) for n, _ in outs],
        out_shape=q_shapes + [jax.ShapeDtypeStruct((t, n), dt) for n, dt in outs],
        compiler_params=_cparams(2),
        name="in_projection",
    )(src.ctx, src.lat, mods, lw["norm_mix"], lw["w_in"], lw["q_norm"], lw["kv_norm"], lw["wq"], lw["wkv"],
      rope)


def _attn_head(q, k, v):
    s = lax.dot_general(q, k, (((1,), (1,)), ((), ())), preferred_element_type=F32)
    p = jnp.exp2(s - jnp.max(s, axis=-1, keepdims=True)).astype(BF16)
    oe = _dot(p, v)
    return oe[:, :V_DIM] / oe[:, V_DIM:V_DIM + 1]


def _attn_kernel(q_ref, k_ref, v_ref, o_ref, *, sub_rows):
    for t in range(q_ref.shape[0] // sub_rows):
        rows = slice(t * sub_rows, (t + 1) * sub_rows)
        outs = []
        for h in range(N_HEADS):
            sl = slice(h * HEAD_PAD, (h + 1) * HEAD_PAD)
            outs.append(_attn_head(q_ref[rows, sl], k_ref[:, sl], v_ref[:, sl]))
        o_ref[rows, :] = jnp.concatenate(outs, axis=-1).astype(BF16)


def _attn_call(q, k, v, nb, n_q, rows_q, sub_rows, rows_kv, kv_stride, name):
    nq = N_HEADS * HEAD_PAD
    nv = N_HEADS * V_DIM
    kv_map = lambda b, i: (b * (kv_stride // rows_kv), 0)
    kv_spec = pl.BlockSpec((rows_kv, nq), kv_map)
    return pl.pallas_call(
        functools.partial(_attn_kernel, sub_rows=sub_rows),
        grid=(nb, n_q),
        in_specs=[pl.BlockSpec((rows_q, nq), lambda b, i: (b * n_q + i, 0)), kv_spec, kv_spec],
        out_specs=pl.BlockSpec((rows_q, nv), lambda b, i: (b * n_q + i, 0)),
        out_shape=jax.ShapeDtypeStruct((nb * n_q * rows_q, nv), BF16),
        compiler_params=_cparams(2),
        name=name,
    )(q, k, v)


def _gelu_tanh(x):
    return 0.5 * x * (1.0 + jnp.tanh(math.sqrt(2.0 / math.pi) * (x + 0.044715 * (x * x * x))))


def _merge_mlp_kernel(ctx_ref, lat_ref, moda_ref, modm_ref, oc_ref, ol_ref, cb_ref, uc_ref, ucp_ref, ucn_ref, ys_ref,
                      gate_ref, wo_ref, cw_ref, cwo_ref, wglu_ref, wout_ref, gm_ref, w1_ref, w2_ref, gf_ref, out_ref,
                      *, first_tile, tps, final):
    i = pl.program_id(1) + first_tile
    uc = uc_ref[...].astype(F32)
    prev_row = jnp.where(i >= 2, ucp_ref[HALO - 1:HALO, :].astype(F32), 0.0)
    next_row = jnp.where(jnp.logical_and(i >= 1, i < tps - 1), ucn_ref[0:1, :].astype(F32), 0.0)
    row = lax.broadcasted_iota(jnp.int32, (TM, 1), 0)
    up = jnp.where(row == 0, prev_row, pltpu.roll(uc, 1, axis=0))
    dn = jnp.where(row == TM - 1, next_row, pltpu.roll(uc, TM - 1, axis=0))
    y = up * cw_ref[0:1, :] + uc * cw_ref[1:2, :] + dn * cw_ref[2:3, :]
    att = _dot(jnp.where(i == 0, oc_ref[...], ol_ref[...]), wo_ref[...])
    conv = _dot((cb_ref[...].astype(F32) * y).astype(BF16), cwo_ref[...])
    z = _dot(_gelu_tanh(ys_ref[...]).astype(BF16), wglu_ref[...])
    s5o = z[:, :D_MODEL] * _sigmoid(z[:, D_MODEL:])
    g = gate_ref[...].astype(F32)
    merged = g[:, :D_MODEL] * att + g[:, D_MODEL:2 * D_MODEL] * conv + g[:, 2 * D_MODEL:] * s5o
    x = jnp.where(i == 0, ctx_ref[...], lat_ref[...])
    x = x + moda_ref[0] * _dot(merged.astype(BF16), wout_ref[...])
    mod = modm_ref[0]
    h = (_rms(x, gm_ref[...]) * (1.0 + mod[:, D_MODEL:2 * D_MODEL]) + mod[:, :D_MODEL]).astype(BF16)
    acc = jnp.zeros((TM, D_MODEL), F32)
    for c in range(D_FF // D_MODEL):
        a = jnp.maximum(_dot(h, w1_ref[:, c * D_MODEL:(c + 1) * D_MODEL]), 0.0)
        acc = acc + _dot((a * a).astype(BF16), w2_ref[c * D_MODEL:(c + 1) * D_MODEL, :])
    y = x + mod[:, 2 * D_MODEL:] * acc
    if final:
        y = _rms(y, gf_ref[...])
    out_ref[...] = y


def _merge_mlp_call(src, mods, o_src, cb, uc, ys, gate, lw, norm_final, nb, tps, first_tile, final):
    t = cb.shape[0]
    n_tiles = tps - first_tile
    blk = lambda b, i: b * tps + i + first_tile
    rowb = lambda n: pl.BlockSpec((TM, n), lambda b, i: (blk(b, i), 0))
    mod_row = lambda b, i: jnp.where(i + first_tile == 0, nb, b)
    per_halo = TM // HALO
    last_halo = t // HALO - 1
    return pl.pallas_call(
        functools.partial(_merge_mlp_kernel, first_tile=first_tile, tps=tps, final=final),
        grid=(nb, n_tiles),
        in_specs=_token_specs(src, first_tile) + [
                  pl.BlockSpec((1, 1, D_MODEL), lambda b, i: (mod_row(b, i), 0, 2)),
                  pl.BlockSpec((1, 1, 3 * D_MODEL), lambda b, i: (mod_row(b, i), 0, 1))]
                 + _token_specs(o_src, first_tile, N_HEADS * V_DIM) + [
                  rowb(CONV_WIDTH), rowb(CONV_WIDTH),
                  pl.BlockSpec((HALO, CONV_WIDTH), lambda b, i: (jnp.maximum(blk(b, i) * per_halo - 1, 0), 0)),
                  pl.BlockSpec((HALO, CONV_WIDTH),
                               lambda b, i: (jnp.minimum((blk(b, i) + 1) * per_halo, last_halo), 0)),
                  rowb(S5_WIDTH), rowb(N_BRANCH * D_MODEL),
                  _const_spec((N_HEADS * V_DIM, D_MODEL)), _const_spec((CONV_K, CONV_WIDTH)),
                  _const_spec((CONV_WIDTH, D_MODEL)),
                  _const_spec((S5_WIDTH, 2 * D_MODEL)), _const_spec((D_MODEL, D_MODEL)),
                  _const_spec((1, D_MODEL)), _const_spec((D_MODEL, D_FF)), _const_spec((D_FF, D_MODEL)),
                  _const_spec((1, D_MODEL))],
        out_specs=pl.BlockSpec((TM, D_MODEL), lambda b, i: (b * n_tiles + i, 0)),
        out_shape=jax.ShapeDtypeStruct((nb * n_tiles * TM, D_MODEL), F32),
        compiler_params=_cparams(2),
        name="merge_mlp",
    )(src.ctx, src.lat, mods, mods, o_src.ctx, o_src.lat, cb, uc, uc, uc, ys, gate,
      lw["w_o"], lw["conv_w"], lw["conv_w_out"], lw["w_glu"], lw["w_out"], lw["norm_mlp"], lw["w1"], lw["w2"],
      norm_final)


def _layer_weights(i, w_in, norm_mix, q_norm, w_uq, kv_norm, w_ukv, w_o, conv_w, conv_w_out, s5_d, w_glu,
                   w_out, norm_mlp, w1, w2):
    wi = w_in[i]
    pe = wi[:, OFF_PE:OFF_CB]
    w_in_p = jnp.concatenate(
        [wi[:, OFF_Q:OFF_PE], wi[:, OFF_CB:OFF_G], wi[:, OFF_G:], pe,
         jnp.zeros((D_MODEL, LANES - QK_ROPE), F32)], axis=1).astype(BF16)
    uq = w_uq[i].reshape(Q_LORA, N_HEADS, QK_NOPE + QK_ROPE)
    zpad = jnp.zeros((Q_LORA, N_HEADS, HEAD_PAD - QK_NOPE - QK_ROPE), F32)
    wq = jnp.concatenate([uq, zpad], axis=-1).reshape(Q_LORA, N_HEADS * HEAD_PAD)
    ukv = w_ukv[i].reshape(KV_LORA, N_HEADS, QK_NOPE + V_DIM)
    wk = jnp.concatenate([ukv[..., :QK_NOPE], jnp.zeros((KV_LORA, N_HEADS, HEAD_PAD - QK_NOPE), F32)],
                         axis=-1).reshape(KV_LORA, N_HEADS * HEAD_PAD)
    wv = jnp.concatenate([ukv[..., QK_NOPE:], jnp.zeros((KV_LORA, N_HEADS, HEAD_PAD - V_DIM), F32)],
                         axis=-1).reshape(KV_LORA, N_HEADS * HEAD_PAD)
    return dict(
        w_in=w_in_p, norm_mix=norm_mix[i].reshape(1, D_MODEL),
        q_norm=q_norm[i].reshape(1, Q_LORA), kv_norm=kv_norm[i].reshape(1, KV_LORA),
        wq=wq.astype(BF16),
        wkv=jnp.concatenate([wk, wv], axis=1).astype(BF16),
        w_o=w_o[i].astype(BF16), conv_w=conv_w[i], conv_w_out=conv_w_out[i].astype(BF16),
        s5_d=s5_d[i].reshape(1, S5_WIDTH), w_glu=w_glu[i].astype(BF16), w_out=w_out[i].astype(BF16),
        norm_mlp=norm_mlp[i].reshape(1, D_MODEL), w1=w1[i].astype(BF16), w2=w2[i].astype(BF16))


def _rope_tables(n_ctx, n_tokens):
    rows = n_tokens // GRID_W
    pos = jnp.stack([jnp.repeat(jnp.arange(rows), GRID_W), jnp.tile(jnp.arange(GRID_W), rows)], -1).astype(F32)
    n_freq = QK_ROPE // 4
    inv = ROPE_THETA ** (-jnp.arange(n_freq, dtype=F32) / n_freq)
    ang = pos[:, :, None, None] * inv[None, None, None, :]
    ang = jnp.broadcast_to(ang, (n_tokens, 2, 2, n_freq)).reshape(n_tokens, QK_ROPE)
    cos = jnp.concatenate([jnp.ones((n_ctx, QK_ROPE), F32), jnp.cos(ang)], axis=0)
    sin = jnp.concatenate([jnp.zeros((n_ctx, QK_ROPE), F32), jnp.sin(ang)], axis=0)
    s = n_ctx + n_tokens
    one = jnp.ones((s, QK_NOPE), F32)
    z = lambda n: jnp.zeros((s, n), F32)
    pad = HEAD_PAD - QK_NOPE - QK_ROPE
    return jnp.concatenate([one, cos, z(pad), z(QK_NOPE), sin, z(pad),
                            cos, z(LANES - QK_ROPE), sin, z(LANES - QK_ROPE)], axis=1)


def kernel(x, c, ctx, c_ctx, ada_w, ada_b, norm_mix, w_in, mla_q_norm, mla_w_uq, mla_kv_norm, mla_w_ukv, mla_w_o, conv_w, conv_w_out, s5_a_re, s5_a_im, s5_log_dt, s5_b_re, s5_b_im, s5_c_re, s5_c_im, s5_d, s5_w_glu, w_out, norm_mlp, mlp_w1, mlp_w2, norm_final):
    nb, n_lat, _ = x.shape
    n_ctx = ctx.shape[1]
    depth = ada_w.shape[0]
    assert n_ctx == TM and n_lat % ATTN_STEP_ROWS == 0 and n_lat % GRID_W == 0 and nb == SUBLANES
    assert (n_ctx + n_lat) % (S5_CHUNK * S5_ASM_ROWS) == 0
    s = n_ctx + n_lat
    tps = s // TM
    t = nb * s
    n_chunks = s // S5_CHUNK
    n_ctx_chunks = n_ctx // S5_CHUNK

    c16 = jnp.zeros((16, D_MODEL), F32).at[:nb].set(c).at[nb].set(c_ctx)
    mods = _ada_call(c16, ada_w, ada_b)[:, :nb + 1].reshape(depth, nb + 1, 1, N_MOD * D_MODEL)
    ops = _s5_operators(s5_a_re, s5_a_im, s5_log_dt, s5_b_re, s5_b_im, s5_c_re, s5_c_im)
    rope = _rope_tables(n_ctx, n_lat)
    src = _TokenSource(ctx.reshape(nb * n_ctx, D_MODEL), x.reshape(nb * n_lat, D_MODEL), 1, tps - 1, 0)
    gf = norm_final.reshape(1, D_MODEL)

    for i in range(depth):
        last = i == depth - 1
        ft = 1 if last else 0
        lw = _layer_weights(i, w_in, norm_mix, mla_q_norm, mla_w_uq, mla_kv_norm, mla_w_ukv, mla_w_o, conv_w,
                            conv_w_out, s5_d, s5_w_glu, w_out, norm_mlp, mlp_w1, mlp_w2)
        q_ctx, q_lat, k, v, cb, uc, s5u, gate = _inproj_call(src, mods[i], lw, rope, nb, tps)
        o_lat = _attn_call(q_lat, k, v, nb, n_lat // ATTN_STEP_ROWS, ATTN_STEP_ROWS, TQ, s, s, "attention")
        o_ctx = o_lat if last else _attn_call(q_ctx, k, v, nb, 1, TM, TM, n_ctx, s, "attention_ctx")
        o_src = _TokenSource(o_ctx, o_lat, 1, tps - 1, 0)
        ys = _s5_call(s5u, ops, lw["s5_d"], i, nb, n_chunks, n_ctx_chunks)
        xs = _merge_mlp_call(src, mods[i], o_src, cb, uc, ys, gate, lw, gf, nb, tps, ft, last)
        src = _TokenSource(xs, xs, tps, tps, 1)
    return xs.reshape(nb, n_lat, D_MODEL)
```
